```python
import math
import jax, jax.numpy as jnp
from jax import lax
import numpy as np

D_MODEL = 2048
BATCH = 4
SEQ = 4096
DEPTH = 1
DEC_BATCH = 2
DEC_SEQ = 16384
PAST_LEN = 128

GRID_W = 64

NA_HEADS = 16
NA_HEAD_DIM = 128
NA_WIDTH = NA_HEADS * NA_HEAD_DIM
NA_WIN_H = 8
NA_WIN_W = 16

RET_HEADS = 8
RET_QK_DIM = 256
RET_V_DIM = 512
RET_QK_WIDTH = RET_HEADS * RET_QK_DIM
RET_V_WIDTH = RET_HEADS * RET_V_DIM
RET_CHUNK = 128
ROPE_BASE = 10000.0

PEER_HEADS = 8
PEER_N_KEYS = 128
PEER_N_EXPERTS = PEER_N_KEYS * PEER_N_KEYS
PEER_QUERY_DIM = 256
PEER_HALF = PEER_QUERY_DIM // 2
PEER_TOPK = 16
PEER_BLOCK = 128

EPS = 1e-6

IN_COLS = 3 * NA_WIDTH + 2 * RET_QK_WIDTH + 2 * RET_V_WIDTH + 2 * D_MODEL
IN_SPLITS = (
    NA_WIDTH,
    2 * NA_WIDTH,
    3 * NA_WIDTH,
    3 * NA_WIDTH + RET_QK_WIDTH,
    3 * NA_WIDTH + 2 * RET_QK_WIDTH,
    3 * NA_WIDTH + 2 * RET_QK_WIDTH + RET_V_WIDTH,
    3 * NA_WIDTH + 2 * RET_QK_WIDTH + 2 * RET_V_WIDTH,
    3 * NA_WIDTH + 2 * RET_QK_WIDTH + 2 * RET_V_WIDTH + D_MODEL,
)

kernel_name = 'hybrid_natten_retnet_peer_encoder'


def rms_norm(x, gain):
    xf = x.astype(jnp.float32)
    y = xf * lax.rsqrt(jnp.mean(xf * xf, axis=-1, keepdims=True) + EPS)
    return (y * gain.astype(jnp.float32)).astype(x.dtype)


def rotary(x, pos):
    d = x.shape[-1]
    half = d // 2
    inv = ROPE_BASE ** (-jnp.arange(half, dtype=jnp.float32) / half)
    ang = pos.astype(jnp.float32)[:, None] * inv[None, :]
    cos = jnp.cos(ang)[None, :, None, :]
    sin = jnp.sin(ang)[None, :, None, :]
    xf = x.astype(jnp.float32)
    x1, x2 = xf[..., :half], xf[..., half:]
    return jnp.concatenate([x1 * cos - x2 * sin, x1 * sin + x2 * cos], axis=-1)


def neighbourhood_attention(q, k, v, rel_bias):
    b, t, h, dh = q.shape
    rows = t // GRID_W
    kh = min(NA_WIN_H, rows)
    kw = NA_WIN_W
    qg = q.reshape(b, rows, GRID_W, h, dh)
    kg = k.reshape(b, rows, GRID_W, h, dh)
    vg = v.reshape(b, rows, GRID_W, h, dh)
    cols = np.arange(GRID_W)
    col_start = np.clip(cols - kw // 2, 0, GRID_W - kw)
    col_idx = col_start[:, None] + np.arange(kw)[None, :]
    dc = col_idx - cols[:, None] + (NA_WIN_W - 1)
    scale = dh ** -0.5

    def one_row(r):
        r0 = jnp.clip(r - kh // 2, 0, rows - kh)
        q_r = lax.dynamic_index_in_dim(qg, r, axis=1, keepdims=False)
        k_rows = lax.dynamic_slice_in_dim(kg, r0, kh, axis=1)
        v_rows = lax.dynamic_slice_in_dim(vg, r0, kh, axis=1)
        k_win = k_rows[:, :, col_idx]
        v_win = v_rows[:, :, col_idx]
        dr = r0 + jnp.arange(kh) - r + (NA_WIN_H - 1)
        bias = rel_bias[:, dr][:, :, dc]
        bias = bias.transpose(0, 2, 1, 3)[None].astype(jnp.float32)
        s = jnp.einsum('bqhd,biqjhd->bhqij', q_r, k_win).astype(jnp.float32) * scale + bias
        p = jax.nn.softmax(s.reshape(b, h, GRID_W, kh * kw), axis=-1).reshape(b, h, GRID_W, kh, kw)
        return jnp.einsum('bhqij,biqjhd->bqhd', p.astype(v.dtype), v_win)

    out = lax.map(one_row, jnp.arange(rows))
    return out.transpose(1, 0, 2, 3, 4).reshape(b, t, h * dh)


def retention_one_direction(q, k, v, log_gamma, include_diag):
    b, h, t, dk = q.shape
    dv = v.shape[-1]
    n_chunks = t // RET_CHUNK
    pos = jnp.arange(RET_CHUNK, dtype=jnp.float32)
    diff = pos[:, None] - pos[None, :]
    mask = (diff >= 0) if include_diag else (diff > 0)
    intra = jnp.exp(jnp.where(mask[None], diff[None] * log_gamma[:, None, None], -jnp.inf))
    q_dec = jnp.exp((pos + 1.0)[None, :] * log_gamma[:, None])[None, :, :, None]
    k_dec = jnp.exp((RET_CHUNK - 1.0 - pos)[None, :] * log_gamma[:, None])[None, :, :, None]
    chunk_dec = jnp.exp(RET_CHUNK * log_gamma)[None, :, None, None]

    def to_chunks(a):
        return a.reshape(b, h, n_chunks, RET_CHUNK, a.shape[-1]).transpose(2, 0, 1, 3, 4)

    def step(state, qkv):
        q_i, k_i, v_i = qkv
        scores = jnp.einsum('bhad,bhcd->bhac', q_i, k_i) * intra[None]
        out = (jnp.einsum('bhac,bhce->bhae', scores, v_i)
               + jnp.einsum('bhad,bhde->bhae', q_i * q_dec, state))
        state = state * chunk_dec + jnp.einsum('bhcd,bhce->bhde', k_i * k_dec, v_i)
        return state, out

    state0 = jnp.zeros((b, h, dk, dv), jnp.float32)
    _, out = lax.scan(step, state0, (to_chunks(q), to_chunks(k), to_chunks(v)))
    return out.transpose(1, 2, 0, 3, 4).reshape(b, h, t, dv)


def bidirectional_retention(r_q, r_k, r_v, r_g, decay_logit, gn_gain):
    b, t, _ = r_q.shape
    pos = jnp.arange(t)
    q = rotary(r_q.reshape(b, t, RET_HEADS, RET_QK_DIM), pos).transpose(0, 2, 1, 3)
    k = (rotary(r_k.reshape(b, t, RET_HEADS, RET_QK_DIM), pos) * (RET_QK_DIM ** -0.5)).transpose(0, 2, 1, 3)
    v = r_v.reshape(b, t, RET_HEADS, RET_V_DIM).astype(jnp.float32).transpose(0, 2, 1, 3)
    log_gamma = jax.nn.log_sigmoid(decay_logit.astype(jnp.float32))
    fwd = retention_one_direction(q, k, v, log_gamma[0], True)
    bwd = jnp.flip(retention_one_direction(jnp.flip(q, 2), jnp.flip(k, 2), jnp.flip(v, 2),
                                           log_gamma[1], False), 2)
    o = fwd + bwd
    mu = jnp.mean(o, axis=-1, keepdims=True)
    var = jnp.mean(jnp.square(o - mu), axis=-1, keepdims=True)
    o = (o - mu) * lax.rsqrt(var + EPS)
    o = o.transpose(0, 2, 1, 3).reshape(b, t, RET_V_WIDTH) * gn_gain.astype(jnp.float32)
    return (jax.nn.silu(r_g.astype(jnp.float32)) * o).astype(r_q.dtype)


def peer(h, w_query, keys_1, keys_2, expert_down, expert_up):
    b, t, d = h.shape
    n = b * t
    x = h.reshape(n, d)
    qry = (x @ w_query).reshape(n, PEER_HEADS, 2, PEER_HALF)
    s1 = jnp.einsum('nhd,hkd->nhk', qry[:, :, 0], keys_1).astype(jnp.float32)
    s2 = jnp.einsum('nhd,hkd->nhk', qry[:, :, 1], keys_2).astype(jnp.float32)
    v1, i1 = lax.top_k(s1, PEER_TOPK)
    v2, i2 = lax.top_k(s2, PEER_TOPK)
    cand_s = (v1[..., :, None] + v2[..., None, :]).reshape(n, PEER_HEADS, PEER_TOPK * PEER_TOPK)
    cand_i = (i1[..., :, None] * PEER_N_KEYS + i2[..., None, :]).reshape(n, PEER_HEADS, PEER_TOPK * PEER_TOPK)
    top_s, sel = lax.top_k(cand_s, PEER_TOPK)
    idx = jnp.take_along_axis(cand_i, sel, axis=-1)
    gate = jax.nn.softmax(top_s, axis=-1)
    n_blk = n // PEER_BLOCK

    def block(args):
        xb, ib, gb = args
        u = expert_down[ib]
        w = expert_up[ib]
        act = jax.nn.gelu(jnp.einsum('nd,nhkd->nhk', xb, u).astype(jnp.float32))
        return jnp.einsum('nhk,nhkd->nd', (gb * act).astype(xb.dtype), w)

    y = lax.map(block, (x.reshape(n_blk, PEER_BLOCK, d),
                        idx.reshape(n_blk, PEER_BLOCK, PEER_HEADS, PEER_TOPK),
                        gate.reshape(n_blk, PEER_BLOCK, PEER_HEADS, PEER_TOPK)))
    return y.reshape(b, t, d)


def encoder_layer(x, c, ada_w, ada_b, norm1_gain, w_in, na_rel_bias, ret_decay_logit, ret_gn_gain,
                  w_na_proj, w_ret_proj, w_out, norm2_gain, peer_query, peer_keys_1, peer_keys_2,
                  peer_down, peer_up):
    b, t, _ = x.shape
    mod = (jax.nn.silu(c) @ ada_w + ada_b)[:, None, :]
    shift1, scale1, gate1, shift2, scale2, gate2 = jnp.split(mod, 6, axis=-1)
    h = rms_norm(x, norm1_gain) * (1.0 + scale1) + shift1
    proj = h @ w_in
    na_q, na_k, na_v, r_q, r_k, r_v, r_g, g_na, g_ret = jnp.split(proj, IN_SPLITS, axis=-1)
    na_shape = (b, t, NA_HEADS, NA_HEAD_DIM)
    na_out = neighbourhood_attention(na_q.reshape(na_shape), na_k.reshape(na_shape),
                                     na_v.reshape(na_shape), na_rel_bias)
    ret_out = bidirectional_retention(r_q, r_k, r_v, r_g, ret_decay_logit, ret_gn_gain)
    merged = (jax.nn.sigmoid(g_na) * (na_out @ w_na_proj)
              + jax.nn.sigmoid(g_ret) * (ret_out @ w_ret_proj))
    x = x + gate1 * (merged @ w_out)
    h2 = rms_norm(x, norm2_gain) * (1.0 + scale2) + shift2
    x = x + gate2 * peer(h2, peer_query, peer_keys_1, peer_keys_2, peer_down, peer_up)
    return x


def trunk(x, c, ada_w, ada_b, norm1_gain, w_in, na_rel_bias, ret_decay_logit, ret_gn_gain,
          w_na_proj, w_ret_proj, w_out, norm2_gain, peer_query, peer_keys_1, peer_keys_2,
          peer_down, peer_up, final_gain):
    for l in range(DEPTH):
        x = encoder_layer(x, c, ada_w[l], ada_b[l], norm1_gain[l], w_in[l], na_rel_bias[l],
                          ret_decay_logit[l], ret_gn_gain[l], w_na_proj[l], w_ret_proj[l], w_out[l],
                          norm2_gain[l], peer_query[l], peer_keys_1[l], peer_keys_2[l],
                          peer_down[l], peer_up[l])
    return rms_norm(x, final_gain)


def setup_inputs(seed: int = 0) -> dict:
    key = jax.random.key(seed)
    ks = jax.random.split(key, 24)
    f32 = jnp.float32
    nrm = lambda k, shape, s: jax.random.normal(k, shape, f32) * s
    gamma0 = 1.0 - 2.0 ** (-5.0 - np.arange(RET_HEADS, dtype=np.float32))
    logit0 = jnp.asarray(np.log(gamma0) - np.log1p(-gamma0), f32)
    decay_logit = logit0[None, None, :] + nrm(ks[8], (DEPTH, 2, RET_HEADS), 0.1)
    return {
        'x_prompt': nrm(ks[0], (BATCH, SEQ, D_MODEL), 1.0),
        'x_sample': nrm(ks[1], (DEC_BATCH, DEC_SEQ, D_MODEL), 1.0),
        'c_prompt': nrm(ks[2], (BATCH, D_MODEL), 1.0),
        'c_sample': nrm(ks[3], (DEC_BATCH, D_MODEL), 1.0),
        'ada_w': nrm(ks[4], (DEPTH, D_MODEL, 6 * D_MODEL), 0.5 * D_MODEL ** -0.5),
        'ada_b': nrm(ks[5], (DEPTH, 6 * D_MODEL), 0.02),
        'norm1_gain': 1.0 + nrm(ks[6], (DEPTH, D_MODEL), 0.02),
        'w_in': nrm(ks[7], (DEPTH, D_MODEL, IN_COLS), D_MODEL ** -0.5),
        'na_rel_bias': nrm(ks[9], (DEPTH, NA_HEADS, 2 * NA_WIN_H - 1, 2 * NA_WIN_W - 1), 0.1),
        'ret_decay_logit': decay_logit,
        'ret_gn_gain': 1.0 + nrm(ks[10], (DEPTH, RET_V_WIDTH), 0.02),
        'w_na_proj': nrm(ks[11], (DEPTH, NA_WIDTH, D_MODEL), NA_WIDTH ** -0.5),
        'w_ret_proj': nrm(ks[12], (DEPTH, RET_V_WIDTH, D_MODEL), RET_V_WIDTH ** -0.5),
        'w_out': nrm(ks[13], (DEPTH, D_MODEL, D_MODEL), D_MODEL ** -0.5),
        'norm2_gain': 1.0 + nrm(ks[14], (DEPTH, D_MODEL), 0.02),
        'peer_query': nrm(ks[15], (DEPTH, D_MODEL, PEER_HEADS * PEER_QUERY_DIM), D_MODEL ** -0.5),
        'peer_keys_1': nrm(ks[16], (DEPTH, PEER_HEADS, PEER_N_KEYS, PEER_HALF), PEER_HALF ** -0.5),
        'peer_keys_2': nrm(ks[17], (DEPTH, PEER_HEADS, PEER_N_KEYS, PEER_HALF), PEER_HALF ** -0.5),
        'peer_down': nrm(ks[18], (DEPTH, PEER_N_EXPERTS, D_MODEL), D_MODEL ** -0.5),
        'peer_up': nrm(ks[19], (DEPTH, PEER_N_EXPERTS, D_MODEL), PEER_HEADS ** -0.5),
        'final_gain': 1.0 + nrm(ks[20], (D_MODEL,), 0.02),
    }


def reference(x_prompt, x_sample, c_prompt, c_sample, ada_w, ada_b, norm1_gain, w_in, na_rel_bias,
              ret_decay_logit, ret_gn_gain, w_na_proj, w_ret_proj, w_out, norm2_gain, peer_query,
              peer_keys_1, peer_keys_2, peer_down, peer_up, final_gain):
    y_prompt = trunk(x_prompt, c_prompt, ada_w, ada_b, norm1_gain, w_in, na_rel_bias, ret_decay_logit,
                     ret_gn_gain, w_na_proj, w_ret_proj, w_out, norm2_gain, peer_query, peer_keys_1,
                     peer_keys_2, peer_down, peer_up, final_gain)
    y_sample = trunk(x_sample, c_sample, ada_w, ada_b, norm1_gain, w_in, na_rel_bias, ret_decay_logit,
                     ret_gn_gain, w_na_proj, w_ret_proj, w_out, norm2_gain, peer_query, peer_keys_1,
                     peer_keys_2, peer_down, peer_up, final_gain)
    return (y_prompt, y_sample)
```

```python
import functools
import math

import numpy as np
import jax
import jax.numpy as jnp
from jax import lax
from jax.experimental import pallas as pl
from jax.experimental.pallas import tpu as pltpu

GRID_W = 64
PEER_TOPK = 16
ROPE_BASE = 10000.0
EPS = 1e-6
RET_CHUNK = 128
NEG = -1e30

F32 = jnp.float32
BF16 = jnp.bfloat16

VMEM_LIMIT = 56 * 1024 * 1024

_NT = (((1,), (1,)), ((), ()))
_TN = (((0,), (0,)), ((), ()))


def _params(*sem):
    return pltpu.CompilerParams(dimension_semantics=sem, vmem_limit_bytes=VMEM_LIMIT)


def _pick(n, prefs):
    for p in prefs:
        if n % p == 0:
            return p
    return n


def _adaln_kernel(c_ref, w_ref, b_ref, o_ref):
    c = c_ref[...]
    s = c * jax.nn.sigmoid(c)
    o_ref[...] = jnp.dot(s, w_ref[...], preferred_element_type=F32,
                         precision=lax.Precision.HIGHEST) + b_ref[...]


def _adaln(c, ada_w, ada_b):
    m, d = c.shape
    n = ada_w.shape[1]
    tn = _pick(n, (1024, 512, 256, 128))
    return pl.pallas_call(
        _adaln_kernel,
        grid=(n // tn,),
        in_specs=[pl.BlockSpec((m, d), lambda j: (0, 0)),
                  pl.BlockSpec((d, tn), lambda j: (0, j)),
                  pl.BlockSpec((1, tn), lambda j: (0, j))],
        out_specs=pl.BlockSpec((m, tn), lambda j: (0, j)),
        out_shape=jax.ShapeDtypeStruct((m, n), F32),
        compiler_params=_params("arbitrary"),
        name="adaln",
    )(c, ada_w, ada_b.reshape(1, n))


def _modulated_norm(x, gain, shift, scale):
    ms = jnp.mean(x * x, axis=-1, keepdims=True)
    return (x * lax.rsqrt(ms + EPS) * gain) * (1.0 + scale) + shift


def _inproj_kernel(x_ref, mod_ref, g_ref, cos_ref, sin_ref, w_ref, o_ref, h_ref, *,
                   tn, dk, t_rq, t_rk, t_rv, t_rg, t_gate, k_scale):
    j = pl.program_id(2)

    @pl.when(j == 0)
    def _():
        h = _modulated_norm(x_ref[...], g_ref[...], mod_ref[0:1, :], mod_ref[1:2, :])
        h_ref[...] = h.astype(BF16)

    acc = jnp.dot(h_ref[...], w_ref[...], preferred_element_type=F32)
    half = dk // 2

    def rotary(mult):
        cos = cos_ref[...]
        sin = sin_ref[...]
        for hh in range(tn // dk):
            a = acc[:, hh * dk:hh * dk + half]
            b = acc[:, hh * dk + half:(hh + 1) * dk]
            o_ref[:, hh * dk:hh * dk + half] = ((a * cos - b * sin) * mult).astype(o_ref.dtype)
            o_ref[:, hh * dk + half:(hh + 1) * dk] = ((a * sin + b * cos) * mult).astype(o_ref.dtype)

    @pl.when((j < t_rq) | ((j >= t_rv) & (j < t_rg)))
    def _():
        o_ref[...] = acc.astype(o_ref.dtype)

    @pl.when((j >= t_rq) & (j < t_rk))
    def _():
        rotary(1.0)

    @pl.when((j >= t_rk) & (j < t_rv))
    def _():
        rotary(k_scale)

    @pl.when((j >= t_rg) & (j < t_gate))
    def _():
        o_ref[...] = (acc * jax.nn.sigmoid(acc)).astype(o_ref.dtype)

    @pl.when(j >= t_gate)
    def _():
        o_ref[...] = jax.nn.sigmoid(acc).astype(o_ref.dtype)


def _inproj(x, mod, gain, cos, sin, w_bf, dims):
    b, t, d = x.shape
    n = w_bf.shape[1]
    na_w, qk_w, v_w, dk = dims["na_w"], dims["qk_w"], dims["v_w"], dims["dk"]
    seg = (3 * na_w, qk_w, qk_w, v_w, v_w, 2 * d)
    tn = 1024
    while any(s % tn for s in seg):
        tn //= 2
    assert tn % dk == 0
    tm = _pick(t, (1024, 512, 256, 128))
    bounds = np.cumsum(seg) // tn
    kern = functools.partial(_inproj_kernel, tn=tn, dk=dk, t_rq=int(bounds[0]), t_rk=int(bounds[1]),
                             t_rv=int(bounds[2]), t_rg=int(bounds[3]), t_gate=int(bounds[4]),
                             k_scale=float(dk) ** -0.5)
    return pl.pallas_call(
        kern,
        grid=(b, t // tm, n // tn),
        in_specs=[pl.BlockSpec((None, tm, d), lambda bi, i, j: (bi, i, 0)),
                  pl.BlockSpec((None, 6, d), lambda bi, i, j: (bi, 0, 0)),
                  pl.BlockSpec((1, d), lambda bi, i, j: (0, 0)),
                  pl.BlockSpec((tm, dk // 2), lambda bi, i, j: (i, 0)),
                  pl.BlockSpec((tm, dk // 2), lambda bi, i, j: (i, 0)),
                  pl.BlockSpec((d, tn), lambda bi, i, j: (0, j))],
        out_specs=pl.BlockSpec((None, tm, tn), lambda bi, i, j: (bi, i, j)),
        out_shape=jax.ShapeDtypeStruct((b, t, n), BF16),
        scratch_shapes=[pltpu.VMEM((tm, d), BF16)],
        compiler_params=_params("arbitrary", "arbitrary", "arbitrary"),
        name="inproj",
    )(x, mod, gain.reshape(1, d), cos, sin, w_bf)


def _na_bias_table(rel_bias, kh):
    nh = rel_bias.shape[0]
    win_h = (rel_bias.shape[1] + 1) // 2
    win_w = (rel_bias.shape[2] + 1) // 2
    cols = np.arange(GRID_W)
    col_start = np.clip(cols - win_w // 2, 0, GRID_W - win_w)
    in_win = (cols[None, :] >= col_start[:, None]) & (cols[None, :] < col_start[:, None] + win_w)
    dc = np.clip(cols[None, :] - cols[:, None] + win_w - 1, 0, 2 * win_w - 2)
    dr = np.arange(kh)[:, None] - (kh - 1) + np.arange(kh)[None, :] + win_h - 1
    tab = rel_bias.astype(F32)[:, dr][:, :, :, dc]
    tab = jnp.where(in_win[None, None, None], tab, NEG)
    return tab.transpose(0, 1, 3, 2, 4).reshape(nh, kh, GRID_W, kh * GRID_W)


def _na_kernel(q_ref, k_ref, v_ref, b_ref, o_ref, *, rows, kh, rpb, scale):
    rb = pl.program_id(2)
    w = GRID_W
    for j in range(rpb):
        r = rb * rpb + j
        r0 = jnp.clip(r - kh // 2, 0, rows - kh)
        start = pl.multiple_of(r0 * w, w)
        q = q_ref[j * w:(j + 1) * w, :]
        kwin = k_ref[pl.ds(start, kh * w), :]
        vwin = v_ref[pl.ds(start, kh * w), :]
        s = lax.dot_general(q, kwin, _NT, preferred_element_type=F32) * scale + b_ref[r0 - r + kh - 1]
        m = jnp.max(s, axis=-1, keepdims=True)
        p = jnp.exp(s - m)
        den = jnp.sum(p, axis=-1, keepdims=True)
        o = jnp.dot(p.astype(BF16), vwin, preferred_element_type=F32) / den
        o_ref[j * w:(j + 1) * w, :] = o.astype(o_ref.dtype)


def _na(proj, bias_tab, dims):
    b, t, _ = proj.shape
    nh, dh, na_w = dims["na_h"], dims["na_dh"], dims["na_w"]
    rows = t // GRID_W
    kh = bias_tab.shape[1]
    rpb = _pick(rows, (8, 4, 2, 1))
    hb = na_w // dh
    kern = functools.partial(_na_kernel, rows=rows, kh=kh, rpb=rpb, scale=float(dh) ** -0.5)
    return pl.pallas_call(
        kern,
        grid=(b, nh, rows // rpb),
        in_specs=[pl.BlockSpec((None, rpb * GRID_W, dh), lambda bi, h, r: (bi, r, h)),
                  pl.BlockSpec((None, t, dh), lambda bi, h, r: (bi, 0, hb + h)),
                  pl.BlockSpec((None, t, dh), lambda bi, h, r: (bi, 0, 2 * hb + h)),
                  pl.BlockSpec((None, kh, GRID_W, kh * GRID_W), lambda bi, h, r: (h, 0, 0, 0))],
        out_specs=pl.BlockSpec((None, rpb * GRID_W, dh), lambda bi, h, r: (bi, r, h)),
        out_shape=jax.ShapeDtypeStruct((b, t, na_w), BF16),
        compiler_params=_params("arbitrary", "arbitrary", "arbitrary"),
        name="natten",
    )(proj, proj, proj, bias_tab)


def _ret_step(lg, q_ref, k_ref, v_ref, state_ref, *, c, backward):
    dk = q_ref.shape[-1]
    row = lax.broadcasted_iota(jnp.int32, (c, c), 0)
    col = lax.broadcasted_iota(jnp.int32, (c, c), 1)
    pos = lax.broadcasted_iota(jnp.int32, (c, dk), 0).astype(F32)
    if backward:
        diff = (col - row).astype(F32)
        intra = jnp.where(col > row, jnp.exp(diff * lg), 0.0)
        q_dec = jnp.exp((c - pos) * lg)
        k_dec = jnp.exp(pos * lg)
    else:
        diff = (row - col).astype(F32)
        intra = jnp.where(row >= col, jnp.exp(diff * lg), 0.0)
        q_dec = jnp.exp((pos + 1.0) * lg)
        k_dec = jnp.exp((c - 1.0 - pos) * lg)
    chunk_dec = jnp.exp(c * lg)
    q = q_ref[...]
    k = k_ref[...]
    v = v_ref[...]
    s = lax.dot_general(q, k, _NT, preferred_element_type=F32) * intra
    qd = (q.astype(F32) * q_dec).astype(BF16)
    kd = (k.astype(F32) * k_dec).astype(BF16)
    st = state_ref[...]
    out = (jnp.dot(s.astype(BF16), v, preferred_element_type=F32)
           + jnp.dot(qd, st.astype(BF16), preferred_element_type=F32))
    state_ref[...] = st * chunk_dec + lax.dot_general(kd, v, _TN, preferred_element_type=F32)
    return out


def _ret_fwd_kernel(lg_ref, q_ref, k_ref, v_ref, o_ref, state_ref, *, c):
    @pl.when(pl.program_id(2) == 0)
    def _():
        state_ref[...] = jnp.zeros_like(state_ref)

    lg = lg_ref[0, pl.program_id(1)]
    o_ref[...] = _ret_step(lg, q_ref, k_ref, v_ref, state_ref, c=c, backward=False)


def _ret_bwd_kernel(lg_ref, q_ref, k_ref, v_ref, f_ref, g_ref, gain_ref, o_ref, state_ref, *, c):
    @pl.when(pl.program_id(2) == 0)
    def _():
        state_ref[...] = jnp.zeros_like(state_ref)

    lg = lg_ref[1, pl.program_id(1)]
    o = _ret_step(lg, q_ref, k_ref, v_ref, state_ref, c=c, backward=True) + f_ref[...]
    mu = jnp.mean(o, axis=-1, keepdims=True)
    d = o - mu
    var = jnp.mean(d * d, axis=-1, keepdims=True)
    o = d * lax.rsqrt(var + EPS) * gain_ref[...]
    o_ref[...] = (g_ref[...].astype(F32) * o).astype(o_ref.dtype)


def _retention(proj, log_gamma, gn_gain, dims):
    b, t, _ = proj.shape
    nh, dk, dv = dims["ret_h"], dims["dk"], dims["dv"]
    na_w, qk_w, v_w = dims["na_w"], dims["qk_w"], dims["v_w"]
    c = RET_CHUNK
    nc = t // c
    assert (3 * na_w) % dk == 0 and qk_w % dk == 0 and (3 * na_w + 2 * qk_w) % dv == 0
    qb = 3 * na_w // dk
    kb = (3 * na_w + qk_w) // dk
    vb = (3 * na_w + 2 * qk_w) // dv
    gb = (3 * na_w + 2 * qk_w + v_w) // dv
    smem = pl.BlockSpec(memory_space=pltpu.SMEM)
    fwd = pl.pallas_call(
        functools.partial(_ret_fwd_kernel, c=c),
        grid=(b, nh, nc),
        in_specs=[smem,
                  pl.BlockSpec((None, c, dk), lambda bi, h, i: (bi, i, qb + h)),
                  pl.BlockSpec((None, c, dk), lambda bi, h, i: (bi, i, kb + h)),
                  pl.BlockSpec((None, c, dv), lambda bi, h, i: (bi, i, vb + h))],
        out_specs=pl.BlockSpec((None, c, dv), lambda bi, h, i: (bi, i, h)),
        out_shape=jax.ShapeDtypeStruct((b, t, v_w), F32),
        scratch_shapes=[pltpu.VMEM((dk, dv), F32)],
        compiler_params=_params("arbitrary", "arbitrary", "arbitrary"),
        name="ret_fwd",
    )(log_gamma, proj, proj, proj)
    return pl.pallas_call(
        functools.partial(_ret_bwd_kernel, c=c),
        grid=(b, nh, nc),
        in_specs=[smem,
                  pl.BlockSpec((None, c, dk), lambda bi, h, i: (bi, nc - 1 - i, qb + h)),
                  pl.BlockSpec((None, c, dk), lambda bi, h, i: (bi, nc - 1 - i, kb + h)),
                  pl.BlockSpec((None, c, dv), lambda bi, h, i: (bi, nc - 1 - i, vb + h)),
                  pl.BlockSpec((None, c, dv), lambda bi, h, i: (bi, nc - 1 - i, h)),
                  pl.BlockSpec((None, c, dv), lambda bi, h, i: (bi, nc - 1 - i, gb + h)),
                  pl.BlockSpec((1, dv), lambda bi, h, i: (0, h))],
        out_specs=pl.BlockSpec((None, c, dv), lambda bi, h, i: (bi, nc - 1 - i, h)),
        out_shape=jax.ShapeDtypeStruct((b, t, v_w), BF16),
        scratch_shapes=[pltpu.VMEM((dk, dv), F32)],
        compiler_params=_params("arbitrary", "arbitrary", "arbitrary"),
        name="ret_bwd",
    )(log_gamma, proj, proj, proj, fwd, proj, gn_gain.reshape(1, v_w))


def _merge_kernel(na_ref, ret_ref, wna_ref, wret_ref, gna_ref, gret_ref, o_ref):
    a = jnp.dot(na_ref[...], wna_ref[...], preferred_element_type=F32)
    r = jnp.dot(ret_ref[...], wret_ref[...], preferred_element_type=F32)
    o_ref[...] = (gna_ref[...].astype(F32) * a + gret_ref[...].astype(F32) * r).astype(o_ref.dtype)


def _merge(na_out, ret_out, proj, w_na_bf, w_ret_bf, dims):
    b, t, na_w = na_out.shape
    v_w = ret_out.shape[-1]
    d = w_na_bf.shape[1]
    tm = _pick(t, (1024, 512, 256, 128))
    tn = _pick(d, (512, 256, 128))
    g0 = (3 * na_w + 2 * dims["qk_w"] + 2 * v_w) // tn
    return pl.pallas_call(
        _merge_kernel,
        grid=(b, t // tm, d // tn),
        in_specs=[pl.BlockSpec((None, tm, na_w), lambda bi, i, j: (bi, i, 0)),
                  pl.BlockSpec((None, tm, v_w), lambda bi, i, j: (bi, i, 0)),
                  pl.BlockSpec((na_w, tn), lambda bi, i, j: (0, j)),
                  pl.BlockSpec((v_w, tn), lambda bi, i, j: (0, j)),
                  pl.BlockSpec((None, tm, tn), lambda bi, i, j: (bi, i, g0 + j)),
                  pl.BlockSpec((None, tm, tn), lambda bi, i, j: (bi, i, g0 + d // tn + j))],
        out_specs=pl.BlockSpec((None, tm, tn), lambda bi, i, j: (bi, i, j)),
        out_shape=jax.ShapeDtypeStruct((b, t, d), BF16),
        compiler_params=_params("arbitrary", "arbitrary", "arbitrary"),
        name="merge",
    )(na_out, ret_out, w_na_bf, w_ret_bf, proj, proj)


def _outproj_kernel(x_ref, m_ref, w_ref, mod_ref, g_ref, x1_ref, h2_ref):
    y = jnp.dot(m_ref[...], w_ref[...], preferred_element_type=F32)
    x1 = x_ref[...] + mod_ref[2:3, :] * y
    x1_ref[...] = x1
    h2_ref[...] = _modulated_norm(x1, g_ref[...], mod_ref[3:4, :], mod_ref[4:5, :]).astype(h2_ref.dtype)


def _outproj(x, merged, w_out_bf, mod, gain2):
    b, t, d = x.shape
    tm = _pick(t, (512, 256, 128))
    return pl.pallas_call(
        _outproj_kernel,
        grid=(b, t // tm),
        in_specs=[pl.BlockSpec((None, tm, d), lambda bi, i: (bi, i, 0)),
                  pl.BlockSpec((None, tm, d), lambda bi, i: (bi, i, 0)),
                  pl.BlockSpec((d, d), lambda bi, i: (0, 0)),
                  pl.BlockSpec((None, 6, d), lambda bi, i: (bi, 0, 0)),
                  pl.BlockSpec((1, d), lambda bi, i: (0, 0))],
        out_specs=[pl.BlockSpec((None, tm, d), lambda bi, i: (bi, i, 0)),
                   pl.BlockSpec((None, tm, d), lambda bi, i: (bi, i, 0))],
        out_shape=[jax.ShapeDtypeStruct((b, t, d), F32), jax.ShapeDtypeStruct((b, t, d), BF16)],
        compiler_params=_params("arbitrary", "arbitrary"),
        name="outproj",
    )(x, merged, w_out_bf, mod, gain2.reshape(1, d))


def _top_rows(x, k):
    rows = x.shape[0]
    iota = lax.broadcasted_iota(jnp.int32, x.shape, 0)
    vals = []
    for _ in range(k):
        m = jnp.max(x, axis=0, keepdims=True)
        first = jnp.min(jnp.where(x == m, iota, rows), axis=0, keepdims=True)
        x = jnp.where(iota == first, -1.0, x)
        vals.append(m)
    return vals


def _route_kernel(h_ref, wq_ref, k1_ref, k2_ref, e1_ref, e2_ref, th_ref, *, topk):
    nh, nk, half = k1_ref.shape
    qt = lax.dot_general(wq_ref[...], h_ref[...], _NT, preferred_element_type=F32).astype(BF16)
    pairs = [(a, b) for a in range(topk) for b in range(topk) if (a + 1) * (b + 1) <= topk]
    for h in range(nh):
        q1 = qt[h * 2 * half:h * 2 * half + half, :]
        q2 = qt[h * 2 * half + half:(h + 1) * 2 * half, :]
        s1 = jnp.dot(k1_ref[h], q1, preferred_element_type=F32)
        s2 = jnp.dot(k2_ref[h], q2, preferred_element_type=F32)
        e1 = jnp.exp(s1 - jnp.max(s1, axis=0, keepdims=True))
        e2 = jnp.exp(s2 - jnp.max(s2, axis=0, keepdims=True))
        v1 = _top_rows(e1, topk)
        v2 = _top_rows(e2, topk)
        cand = jnp.concatenate([v1[a] * v2[b] for a, b in pairs], axis=0)
        best = _top_rows(cand, topk)
        z = best[0]
        for w in best[1:]:
            z = z + w
        rz = 1.0 / z
        selected = cand >= best[-1]
        cand_scaled = jnp.concatenate([v1[a] * (v2[b] * rz) for a, b in pairs], axis=0)
        e1_ref[h] = e1
        e2_ref[h] = e2 * rz
        th_ref[h:h + 1, :] = jnp.min(jnp.where(selected, cand_scaled, 2.0), axis=0, keepdims=True)


def _route(h2, wq_t_bf, k1_bf, k2_bf):
    b, t, d = h2.shape
    nh, nk, half = k1_bf.shape
    qd = wq_t_bf.shape[0]
    tm = _pick(t, (512, 256, 128))
    kern = functools.partial(_route_kernel, topk=PEER_TOPK)
    return pl.pallas_call(
        kern,
        grid=(b, t // tm),
        in_specs=[pl.BlockSpec((None, tm, d), lambda bi, i: (bi, i, 0)),
                  pl.BlockSpec((qd, d), lambda bi, i: (0, 0)),
                  pl.BlockSpec((nh, nk, half), lambda bi, i: (0, 0, 0)),
                  pl.BlockSpec((nh, nk, half), lambda bi, i: (0, 0, 0))],
        out_specs=[pl.BlockSpec((None, nh, nk, tm), lambda bi, i: (bi, 0, 0, i)),
                   pl.BlockSpec((None, nh, nk, tm), lambda bi, i: (bi, 0, 0, i)),
                   pl.BlockSpec((None, nh, tm), lambda bi, i: (bi, 0, i))],
        out_shape=[jax.ShapeDtypeStruct((b, nh, nk, t), F32),
                   jax.ShapeDtypeStruct((b, nh, nk, t), F32),
                   jax.ShapeDtypeStruct((b, nh, t), F32)],
        compiler_params=_params("arbitrary", "arbitrary"),
        name="peer_route",
    )(h2, wq_t_bf, k1_bf, k2_bf)


def _gelu_tanh(x):
    return 0.5 * x * (1.0 + jnp.tanh(math.sqrt(2.0 / math.pi) * (x + 0.044715 * (x * x * x))))


def _peer_kernel(h_ref, dn_ref, up_ref, e1_ref, e2_ref, th_ref, x1_ref, mod_ref, g_ref, o_ref,
                 acc_ref, p_ref, *, lane):
    e = pl.program_id(2)
    nh, nk, tm = e1_ref.shape
    te = dn_ref.shape[0]
    kpt = te // nk

    @pl.when(e == 0)
    def _():
        acc_ref[...] = jnp.zeros_like(acc_ref)

    act = _gelu_tanh(lax.dot_general(dn_ref[...], h_ref[...], _NT, preferred_element_type=F32))
    groups = 8 // kpt
    base = pl.multiple_of((e // groups) * 8, 8)
    sub = e % groups
    for ii in range(kpt):
        for lc in range(tm // lane):
            ls = slice(lc * lane, (lc + 1) * lane)
            g = jnp.zeros((nk, lane), F32)
            for h in range(nh):
                grp = e1_ref[h, pl.ds(base, 8), ls]
                row = grp[ii:ii + 1, :]
                for s in range(1, groups):
                    row = jnp.where(sub == s, grp[s * kpt + ii:s * kpt + ii + 1, :], row)
                w = e2_ref[h, :, ls] * row
                g = g + jnp.where(w >= th_ref[h:h + 1, ls], w, 0.0)
            p_ref[ii * nk:(ii + 1) * nk, ls] = (g * act[ii * nk:(ii + 1) * nk, ls]).astype(p_ref.dtype)
    acc_ref[...] += jnp.dot(up_ref[...], p_ref[...], preferred_element_type=F32)

    @pl.when(e == pl.num_programs(2) - 1)
    def _():
        x2 = x1_ref[...] + mod_ref[5:6, :] * acc_ref[...].T
        ms = jnp.mean(x2 * x2, axis=-1, keepdims=True)
        o_ref[...] = x2 * lax.rsqrt(ms + EPS) * g_ref[...]


def _peer(h2, x1, e1, e2, th, down_bf, up_t_bf, mod, final_gain):
    b, t, d = h2.shape
    nh, nk = e1.shape[1], e1.shape[2]
    ne = down_bf.shape[0]
    tm = _pick(t, (512, 256, 128))
    te = 4 * nk
    assert nk % 8 == 0 and 8 % (te // nk) == 0 and ne == nk * nk
    kern = functools.partial(_peer_kernel, lane=128)
    return pl.pallas_call(
        kern,
        grid=(b, t // tm, ne // te),
        in_specs=[pl.BlockSpec((None, tm, d), lambda bi, i, e: (bi, i, 0)),
                  pl.BlockSpec((te, d), lambda bi, i, e: (e, 0)),
                  pl.BlockSpec((d, te), lambda bi, i, e: (0, e)),
                  pl.BlockSpec((None, nh, nk, tm), lambda bi, i, e: (bi, 0, 0, i)),
                  pl.BlockSpec((None, nh, nk, tm), lambda bi, i, e: (bi, 0, 0, i)),
                  pl.BlockSpec((None, nh, tm), lambda bi, i, e: (bi, 0, i)),
                  pl.BlockSpec((None, tm, d), lambda bi, i, e: (bi, i, 0)),
                  pl.BlockSpec((None, 6, d), lambda bi, i, e: (bi, 0, 0)),
                  pl.BlockSpec((1, d), lambda bi, i, e: (0, 0))],
        out_specs=pl.BlockSpec((None, tm, d), lambda bi, i, e: (bi, i, 0)),
        out_shape=jax.ShapeDtypeStruct((b, t, d), F32),
        scratch_shapes=[pltpu.VMEM((d, tm), F32), pltpu.VMEM((te, tm), BF16)],
        compiler_params=_params("arbitrary", "arbitrary", "arbitrary"),
        name="peer_experts",
    )(h2, down_bf, up_t_bf, e1, e2, th, x1, mod, final_gain.reshape(1, d))


def _rope_tables(t, half):
    inv = ROPE_BASE ** (-jnp.arange(half, dtype=F32) / half)
    ang = jnp.arange(t, dtype=F32)[:, None] * inv[None, :]
    return jnp.cos(ang), jnp.sin(ang)


def _trunk(x, mod, w, dims):
    b, t, d = x.shape
    cos, sin = _rope_tables(t, dims["dk"] // 2)
    proj = _inproj(x, mod, w["norm1_gain"], cos, sin, w["w_in"], dims)
    kh = min(dims["win_h"], t // GRID_W)
    na_out = _na(proj, _na_bias_table(w["na_rel_bias"], kh), dims)
    ret_out = _retention(proj, w["log_gamma"], w["ret_gn_gain"], dims)
    merged = _merge(na_out, ret_out, proj, w["w_na_proj"], w["w_ret_proj"], dims)
    x1, h2 = _outproj(x, merged, w["w_out"], mod, w["norm2_gain"])
    e1, e2, th = _route(h2, w["peer_query_t"], w["peer_keys_1"], w["peer_keys_2"])
    return _peer(h2, x1, e1, e2, th, w["peer_down"], w["peer_up_t"], mod, w["final_gain"])


def kernel(x_prompt, x_sample, c_prompt, c_sample, ada_w, ada_b, norm1_gain, w_in, na_rel_bias,
           ret_decay_logit, ret_gn_gain, w_na_proj, w_ret_proj, w_out, norm2_gain, peer_query,
           peer_keys_1, peer_keys_2, peer_down, peer_up, final_gain):
    assert ada_w.shape[0] == 1, "the final norm is fused into the single layer's last kernel"
    d = x_prompt.shape[-1]
    na_h = na_rel_bias.shape[1]
    na_w = w_na_proj.shape[1]
    v_w = w_ret_proj.shape[1]
    ret_h = ret_decay_logit.shape[-1]
    qk_w = (w_in.shape[2] - 3 * na_w - 2 * v_w - 2 * d) // 2
    dims = dict(na_h=na_h, na_w=na_w, na_dh=na_w // na_h, ret_h=ret_h, qk_w=qk_w, v_w=v_w,
                dk=qk_w // ret_h, dv=v_w // ret_h, win_h=(na_rel_bias.shape[2] + 1) // 2)
    bp, bs = x_prompt.shape[0], x_sample.shape[0]
    pad = (-(bp + bs)) % 8
    c_all = jnp.concatenate([c_prompt, c_sample, jnp.zeros((pad, d), F32)], axis=0)
    mod = _adaln(c_all, ada_w[0], ada_b[0])
    w = dict(
        norm1_gain=norm1_gain[0], w_in=w_in[0].astype(BF16), na_rel_bias=na_rel_bias[0],
        log_gamma=jax.nn.log_sigmoid(ret_decay_logit[0].astype(F32)), ret_gn_gain=ret_gn_gain[0],
        w_na_proj=w_na_proj[0].astype(BF16), w_ret_proj=w_ret_proj[0].astype(BF16),
        w_out=w_out[0].astype(BF16), norm2_gain=norm2_gain[0],
        peer_query_t=peer_query[0].T.astype(BF16), peer_keys_1=peer_keys_1[0].astype(BF16),
        peer_keys_2=peer_keys_2[0].astype(BF16), peer_down=peer_down[0].astype(BF16),
        peer_up_t=peer_up[0].T.astype(BF16), final_gain=final_gain)
    y_prompt = _trunk(x_prompt, mod[:bp].reshape(bp, 6, d), w, dims)
    y_sample = _trunk(x_sample, mod[bp:bp + bs].reshape(bs, 6, d), w, dims)
    return (y_prompt, y_sample)
```

```python
import functools
import math

import numpy as np
import jax
import jax.numpy as jnp
from jax import lax
from jax.experimental import pallas as pl
from jax.experimental.pallas import tpu as pltpu

GRID_W = 64
PEER_TOPK = 16
ROPE_BASE = 10000.0
EPS = 1e-6
RET_CHUNK = 512
NEG = -1e30

F32 = jnp.float32
BF16 = jnp.bfloat16

VMEM_LIMIT = 56 * 1024 * 1024

_NT = (((1,), (1,)), ((), ()))
_TN = (((0,), (0,)), ((), ()))


def _params(*sem):
    return pltpu.CompilerParams(dimension_semantics=sem, vmem_limit_bytes=VMEM_LIMIT)


def _pick(n, prefs):
    for p in prefs:
        if n % p == 0:
            return p
    return n


def _adaln_kernel(c_ref, w_ref, b_ref, o_ref):
    c = c_ref[...]
    s = c * jax.nn.sigmoid(c)
    o_ref[...] = jnp.dot(s, w_ref[...], preferred_element_type=F32,
                         precision=lax.Precision.HIGHEST) + b_ref[...]


def _adaln(c, ada_w, ada_b):
    m, d = c.shape
    n = ada_w.shape[1]
    tn = _pick(n, (1024, 512, 256, 128))
    return pl.pallas_call(
        _adaln_kernel,
        grid=(n // tn,),
        in_specs=[pl.BlockSpec((m, d), lambda j: (0, 0)),
                  pl.BlockSpec((d, tn), lambda j: (0, j)),
                  pl.BlockSpec((1, tn), lambda j: (0, j))],
        out_specs=pl.BlockSpec((m, tn), lambda j: (0, j)),
        out_shape=jax.ShapeDtypeStruct((m, n), F32),
        compiler_params=_params("arbitrary"),
        name="adaln",
    )(c, ada_w, ada_b.reshape(1, n))


def _modulated_norm(x, gain, shift, scale):
    ms = jnp.mean(x * x, axis=-1, keepdims=True)
    return (x * lax.rsqrt(ms + EPS) * gain) * (1.0 + scale) + shift


def _hnorm_kernel(x_ref, mod_ref, g_ref, o_ref):
    o_ref[...] = _modulated_norm(x_ref[...], g_ref[...], mod_ref[0:1, :], mod_ref[1:2, :]).astype(o_ref.dtype)


def _hnorm(x, mod, gain):
    b, t, d = x.shape
    tm = _pick(t, (512, 256, 128))
    return pl.pallas_call(
        _hnorm_kernel,
        grid=(b, t // tm),
        in_specs=[pl.BlockSpec((None, tm, d), lambda bi, i: (bi, i, 0)),
                  pl.BlockSpec((None, 6, d), lambda bi, i: (bi, 0, 0)),
                  pl.BlockSpec((1, d), lambda bi, i: (0, 0))],
        out_specs=pl.BlockSpec((None, tm, d), lambda bi, i: (bi, i, 0)),
        out_shape=jax.ShapeDtypeStruct((b, t, d), BF16),
        compiler_params=_params("arbitrary", "arbitrary"),
        name="hnorm",
    )(x, mod, gain.reshape(1, d))


def _proj_kernel(*refs, kind, dk):
    h_ref, w_ref = refs[0], refs[1]
    o_ref = refs[-1]
    acc = jnp.dot(h_ref[...], w_ref[...], preferred_element_type=F32)
    if kind == "linear":
        o_ref[...] = (acc * refs[2][...]).astype(o_ref.dtype)
    elif kind == "rotary":
        cs_ref, cos_ref, sin_ref = refs[2:5]
        cos = cos_ref[...]
        sin = sin_ref[...]
        half = dk // 2
        for hh in range(acc.shape[1] // dk):
            lo = slice(hh * dk, hh * dk + half)
            hi = slice(hh * dk + half, (hh + 1) * dk)
            a = acc[:, lo]
            b = acc[:, hi]
            o_ref[:, lo] = ((a * cos - b * sin) * cs_ref[:, lo]).astype(o_ref.dtype)
            o_ref[:, hi] = ((a * sin + b * cos) * cs_ref[:, hi]).astype(o_ref.dtype)
    elif kind == "silu":
        o_ref[...] = (acc * jax.nn.sigmoid(acc)).astype(o_ref.dtype)
    else:
        assert kind == "sigmoid"
        o_ref[...] = jax.nn.sigmoid(acc).astype(o_ref.dtype)


def _proj(h, w_bf, col_block, ncols, tn, kind, extras=(), dk=0):
    b, t, d = h.shape
    tm = _pick(t, (1024, 512, 256, 128))
    extra_specs = []
    for arr in extras:
        if arr.shape[0] == 1:
            extra_specs.append(pl.BlockSpec((1, tn), lambda bi, i, j: (0, j)))
        else:
            extra_specs.append(pl.BlockSpec((tm, arr.shape[1]), lambda bi, i, j: (i, 0)))
    return pl.pallas_call(
        functools.partial(_proj_kernel, kind=kind, dk=dk),
        grid=(b, t // tm, ncols // tn),
        in_specs=[pl.BlockSpec((None, tm, d), lambda bi, i, j: (bi, i, 0)),
                  pl.BlockSpec((d, tn), lambda bi, i, j: (0, col_block(j)))] + extra_specs,
        out_specs=pl.BlockSpec((None, tm, tn), lambda bi, i, j: (bi, i, j)),
        out_shape=jax.ShapeDtypeStruct((b, t, ncols), BF16),
        compiler_params=_params("arbitrary", "arbitrary", "arbitrary"),
        name="inproj_" + kind,
    )(h, w_bf, *extras)


def _inproj(h, cos, sin, w_bf, dims):
    d = h.shape[-1]
    na_w, qk_w, v_w, dk = dims["na_w"], dims["qk_w"], dims["v_w"], dims["dk"]
    tn = 1024
    while any(s % tn for s in (na_w, qk_w, v_w, d)):
        tn //= 2
    assert tn % dk == 0
    n_na, n_qk, n_v = 3 * na_w // tn, 2 * qk_w // tn, v_w // tn
    ones = functools.partial(jnp.ones, dtype=F32)
    cs_lin = jnp.concatenate([jnp.full((na_w,), float(dims["na_dh"]) ** -0.5, F32), ones((2 * na_w + v_w,))])
    cs_rot = jnp.concatenate([ones((qk_w,)), jnp.full((qk_w,), float(dk) ** -0.5, F32)])
    qkv_v = _proj(h, w_bf, lambda j: jnp.where(j < n_na, j, j + n_qk), 3 * na_w + v_w, tn, "linear",
                  (cs_lin.reshape(1, -1),))
    rqk = _proj(h, w_bf, lambda j: n_na + j, 2 * qk_w, tn, "rotary", (cs_rot.reshape(1, -1), cos, sin), dk)
    rg = _proj(h, w_bf, lambda j: n_na + n_qk + n_v + j, v_w, tn, "silu")
    gates = _proj(h, w_bf, lambda j: n_na + n_qk + 2 * n_v + j, 2 * d, tn, "sigmoid")
    return qkv_v, rqk, rg, gates


def _na_plan(rows, win_h):
    kh = min(win_h, rows)
    rpb = _pick(rows, (8, 4, 2, 1))
    kr = min(rows, rpb + kh)
    return kh, rpb, kr


def _na_bias_table(rel_bias, rows):
    nh = rel_bias.shape[0]
    win_h = (rel_bias.shape[1] + 1) // 2
    win_w = (rel_bias.shape[2] + 1) // 2
    kh, rpb, kr = _na_plan(rows, win_h)
    sigs, geoms, ids = [], [], []
    for rb in range(rows // rpb):
        kb = int(np.clip(rb * rpb - (kr - rpb) // 2, 0, rows - kr))
        r = rb * rpb + np.arange(rpb)
        r0 = np.clip(r - kh // 2, 0, rows - kh)
        assert (r0 >= kb).all() and (r0 + kh <= kb + kr).all()
        kabs = kb + np.arange(kr)
        valid = (kabs[None, :] >= r0[:, None]) & (kabs[None, :] < r0[:, None] + kh)
        dr = np.clip(kabs[None, :] - r[:, None] + win_h - 1, 0, 2 * win_h - 2)
        sig = (valid.tobytes(), dr.tobytes())
        if sig not in sigs:
            sigs.append(sig)
            geoms.append((valid, dr))
        ids.append(sigs.index(sig))
    masked = 2 * win_h - 1
    idx = np.stack([np.where(g[0], g[1], masked) for g in geoms])
    assert kr % 2 == 0
    pairs = idx.reshape(len(geoms), rpb, kr // 2, 2)
    combos = sorted({(int(a), int(c)) for a, c in pairs.reshape(-1, 2)})
    pair_ids = np.array([combos.index((int(a), int(c))) for a, c in pairs.reshape(-1, 2)], np.int32)
    cols = np.arange(GRID_W)
    col_start = np.clip(cols - win_w // 2, 0, GRID_W - win_w)
    in_win = (cols[None, :] >= col_start[:, None]) & (cols[None, :] < col_start[:, None] + win_w)
    dc = np.clip(cols[None, :] - cols[:, None] + win_w - 1, 0, 2 * win_w - 2)
    tiles = jnp.where(in_win[None, None], rel_bias.astype(F32)[:, :, dc], NEG)
    tiles = jnp.concatenate([tiles, jnp.full((nh, 1, GRID_W, GRID_W), NEG, F32)], axis=1)
    left = tiles[:, np.array([a for a, _ in combos])]
    right = tiles[:, np.array([c for _, c in combos])]
    tab = jnp.concatenate([left, right], axis=-1)
    return tab, jnp.asarray(ids, jnp.int32), jnp.asarray(pair_ids)


def _na_kernel(vid_ref, pid_ref, q_ref, k_ref, v_ref, b_ref, o_ref, *, rows, rpb, kr, bps):
    step = pl.program_id(2)
    w = GRID_W
    npair = kr // 2
    for u in range(bps):
        rb = step * bps + u
        kb = jnp.clip(rb * rpb - (kr - rpb) // 2, 0, rows - kr)
        start = pl.multiple_of(kb * w, w)
        q = q_ref[u * rpb * w:(u + 1) * rpb * w, :]
        kwin = k_ref[pl.ds(start, kr * w), :]
        vwin = v_ref[pl.ds(start, kr * w), :]
        s = lax.dot_general(q, kwin, _NT, preferred_element_type=F32)
        geom = vid_ref[rb] * (rpb * npair)
        probs, dens = [], []
        for j in range(rpb):
            sj = jnp.concatenate(
                [s[j * w:(j + 1) * w, ip * 2 * w:(ip + 1) * 2 * w] + b_ref[pid_ref[geom + j * npair + ip]]
                 for ip in range(npair)], axis=-1)
            m = jnp.max(sj, axis=-1, keepdims=True)
            p = jnp.exp(sj - m)
            dens.append(jnp.sum(p, axis=-1, keepdims=True))
            probs.append(p.astype(BF16))
        o = jnp.dot(jnp.concatenate(probs, axis=0), vwin, preferred_element_type=F32)
        o_ref[u * rpb * w:(u + 1) * rpb * w, :] = (o / jnp.concatenate(dens, axis=0)).astype(o_ref.dtype)


def _na(proj, bias_tab, block_ids, pair_ids, dims):
    b, t, _ = proj.shape
    nh, dh, na_w = dims["na_h"], dims["na_dh"], dims["na_w"]
    rows = t // GRID_W
    _, rpb, kr = _na_plan(rows, dims["win_h"])
    ncomb = bias_tab.shape[1]
    nblk = rows // rpb
    bps = _pick(nblk, (4, 2, 1))
    hb = na_w // dh
    kern = functools.partial(_na_kernel, rows=rows, rpb=rpb, kr=kr, bps=bps)
    qrows = bps * rpb * GRID_W
    grid_spec = pltpu.PrefetchScalarGridSpec(
        num_scalar_prefetch=2,
        grid=(b, nh, nblk // bps),
        in_specs=[pl.BlockSpec((None, qrows, dh), lambda bi, h, r, vid, pid: (bi, r, h)),
                  pl.BlockSpec((None, t, dh), lambda bi, h, r, vid, pid: (bi, 0, hb + h)),
                  pl.BlockSpec((None, t, dh), lambda bi, h, r, vid, pid: (bi, 0, 2 * hb + h)),
                  pl.BlockSpec((None, ncomb, GRID_W, 2 * GRID_W), lambda bi, h, r, vid, pid: (h, 0, 0, 0))],
        out_specs=pl.BlockSpec((None, qrows, dh), lambda bi, h, r, vid, pid: (bi, r, h)))
    return pl.pallas_call(
        kern,
        grid_spec=grid_spec,
        out_shape=jax.ShapeDtypeStruct((b, t, na_w), BF16),
        compiler_params=_params("arbitrary", "arbitrary", "arbitrary"),
        name="natten",
    )(block_ids, pair_ids, proj, proj, proj, bias_tab)


def _ret_cross(lg, q_ref, k_ref, v_ref, state_ref, qdec_ref, kdec_ref):
    c = q_ref.shape[0]
    qd = (q_ref[...].astype(F32) * qdec_ref[...]).astype(BF16)
    kd = (k_ref[...].astype(F32) * kdec_ref[...]).astype(BF16)
    st = state_ref[...]
    out = jnp.dot(qd, st.astype(BF16), preferred_element_type=F32)
    state_ref[...] = st * jnp.exp(c * lg) + lax.dot_general(kd, v_ref[...], _TN, preferred_element_type=F32)
    return out


def _ret_fwd_kernel(lg_ref, q_ref, k_ref, v_ref, o_ref, state_ref, qdec_ref, kdec_ref, intra_ref):
    lg = lg_ref[0, pl.program_id(1)]
    c, dk = qdec_ref.shape

    @pl.when(pl.program_id(2) == 0)
    def _():
        lg_b = lg_ref[1, pl.program_id(1)]
        row = lax.broadcasted_iota(jnp.int32, (c, c), 0)
        col = lax.broadcasted_iota(jnp.int32, (c, c), 1)
        pos = lax.broadcasted_iota(jnp.int32, (c, dk), 0).astype(F32)
        intra_ref[...] = jnp.where(row >= col, jnp.exp((row - col).astype(F32) * lg),
                                   jnp.exp((col - row).astype(F32) * lg_b))
        qdec_ref[...] = jnp.exp((pos + 1.0) * lg)
        kdec_ref[...] = jnp.exp((c - 1.0 - pos) * lg)
        state_ref[...] = jnp.zeros_like(state_ref)

    s = lax.dot_general(q_ref[...], k_ref[...], _NT, preferred_element_type=F32) * intra_ref[...]
    o_ref[...] = (jnp.dot(s.astype(BF16), v_ref[...], preferred_element_type=F32)
                  + _ret_cross(lg, q_ref, k_ref, v_ref, state_ref, qdec_ref, kdec_ref))


def _ret_bwd_kernel(lg_ref, q_ref, k_ref, v_ref, f_ref, g_ref, gain_ref, o_ref, state_ref, qdec_ref, kdec_ref):
    lg = lg_ref[1, pl.program_id(1)]
    c, dk = qdec_ref.shape

    @pl.when(pl.program_id(2) == 0)
    def _():
        pos = lax.broadcasted_iota(jnp.int32, (c, dk), 0).astype(F32)
        qdec_ref[...] = jnp.exp((c - pos) * lg)
        kdec_ref[...] = jnp.exp(pos * lg)
        state_ref[...] = jnp.zeros_like(state_ref)

    o = _ret_cross(lg, q_ref, k_ref, v_ref, state_ref, qdec_ref, kdec_ref) + f_ref[...]
    mu = jnp.mean(o, axis=-1, keepdims=True)
    d = o - mu
    var = jnp.mean(d * d, axis=-1, keepdims=True)
    o = d * lax.rsqrt(var + EPS) * gain_ref[...]
    o_ref[...] = (g_ref[...].astype(F32) * o).astype(o_ref.dtype)


def _retention(qkv_v, rqk, rg, log_gamma, gn_gain, dims):
    b, t, _ = rqk.shape
    nh, dk, dv = dims["ret_h"], dims["dk"], dims["dv"]
    na_w, qk_w, v_w = dims["na_w"], dims["qk_w"], dims["v_w"]
    c = _pick(t, (RET_CHUNK, 256, 128))
    nc = t // c
    scratch = [pltpu.VMEM((dk, dv), F32), pltpu.VMEM((c, dk), F32), pltpu.VMEM((c, dk), F32)]
    assert qk_w % dk == 0 and (3 * na_w) % dv == 0
    kb = qk_w // dk
    vb = 3 * na_w // dv
    smem = pl.BlockSpec(memory_space=pltpu.SMEM)
    fwd = pl.pallas_call(
        _ret_fwd_kernel,
        grid=(b, nh, nc),
        in_specs=[smem,
                  pl.BlockSpec((None, c, dk), lambda bi, h, i: (bi, i, h)),
                  pl.BlockSpec((None, c, dk), lambda bi, h, i: (bi, i, kb + h)),
                  pl.BlockSpec((None, c, dv), lambda bi, h, i: (bi, i, vb + h))],
        out_specs=pl.BlockSpec((None, c, dv), lambda bi, h, i: (bi, i, h)),
        out_shape=jax.ShapeDtypeStruct((b, t, v_w), F32),
        scratch_shapes=scratch + [pltpu.VMEM((c, c), F32)],
        compiler_params=_params("arbitrary", "arbitrary", "arbitrary"),
        name="ret_fwd",
    )(log_gamma, rqk, rqk, qkv_v)
    return pl.pallas_call(
        _ret_bwd_kernel,
        grid=(b, nh, nc),
        in_specs=[smem,
                  pl.BlockSpec((None, c, dk), lambda bi, h, i: (bi, nc - 1 - i, h)),
                  pl.BlockSpec((None, c, dk), lambda bi, h, i: (bi, nc - 1 - i, kb + h)),
                  pl.BlockSpec((None, c, dv), lambda bi, h, i: (bi, nc - 1 - i, vb + h)),
                  pl.BlockSpec((None, c, dv), lambda bi, h, i: (bi, nc - 1 - i, h)),
                  pl.BlockSpec((None, c, dv), lambda bi, h, i: (bi, nc - 1 - i, h)),
                  pl.BlockSpec((1, dv), lambda bi, h, i: (0, h))],
        out_specs=pl.BlockSpec((None, c, dv), lambda bi, h, i: (bi, nc - 1 - i, h)),
        out_shape=jax.ShapeDtypeStruct((b, t, v_w), BF16),
        scratch_shapes=scratch,
        compiler_params=_params("arbitrary", "arbitrary", "arbitrary"),
        name="ret_bwd",
    )(log_gamma, rqk, rqk, qkv_v, fwd, rg, gn_gain.reshape(1, v_w))


def _merge_kernel(na_ref, ret_ref, wna_ref, wret_ref, gna_ref, gret_ref, o_ref):
    a = jnp.dot(na_ref[...], wna_ref[...], preferred_element_type=F32)
    r = jnp.dot(ret_ref[...], wret_ref[...], preferred_element_type=F32)
    o_ref[...] = (gna_ref[...].astype(F32) * a + gret_ref[...].astype(F32) * r).astype(o_ref.dtype)


def _merge(na_out, ret_out, gates, w_na_bf, w_ret_bf):
    b, t, na_w = na_out.shape
    v_w = ret_out.shape[-1]
    d = w_na_bf.shape[1]
    tm = _pick(t, (1024, 512, 256, 128))
    tn = _pick(d, (512, 256, 128))
    return pl.pallas_call(
        _merge_kernel,
        grid=(b, t // tm, d // tn),
        in_specs=[pl.BlockSpec((None, tm, na_w), lambda bi, i, j: (bi, i, 0)),
                  pl.BlockSpec((None, tm, v_w), lambda bi, i, j: (bi, i, 0)),
                  pl.BlockSpec((na_w, tn), lambda bi, i, j: (0, j)),
                  pl.BlockSpec((v_w, tn), lambda bi, i, j: (0, j)),
                  pl.BlockSpec((None, tm, tn), lambda bi, i, j: (bi, i, j)),
                  pl.BlockSpec((None, tm, tn), lambda bi, i, j: (bi, i, d // tn + j))],
        out_specs=pl.BlockSpec((None, tm, tn), lambda bi, i, j: (bi, i, j)),
        out_shape=jax.ShapeDtypeStruct((b, t, d), BF16),
        compiler_params=_params("arbitrary", "arbitrary", "arbitrary"),
        name="merge",
    )(na_out, ret_out, w_na_bf, w_ret_bf, gates, gates)


def _outproj_kernel(x_ref, m_ref, w_ref, mod_ref, g_ref, x1_ref, h2_ref):
    y = jnp.dot(m_ref[...], w_ref[...], preferred_element_type=F32)
    x1 = x_ref[...] + mod_ref[2:3, :] * y
    x1_ref[...] = x1
    h2_ref[...] = _modulated_norm(x1, g_ref[...], mod_ref[3:4, :], mod_ref[4:5, :]).astype(h2_ref.dtype)


def _outproj(x, merged, w_out_bf, mod, gain2):
    b, t, d = x.shape
    tm = _pick(t, (512, 256, 128))
    return pl.pallas_call(
        _outproj_kernel,
        grid=(b, t // tm),
        in_specs=[pl.BlockSpec((None, tm, d), lambda bi, i: (bi, i, 0)),
                  pl.BlockSpec((None, tm, d), lambda bi, i: (bi, i, 0)),
                  pl.BlockSpec((d, d), lambda bi, i: (0, 0)),
                  pl.BlockSpec((None, 6, d), lambda bi, i: (bi, 0, 0)),
                  pl.BlockSpec((1, d), lambda bi, i: (0, 0))],
        out_specs=[pl.BlockSpec((None, tm, d), lambda bi, i: (bi, i, 0)),
                   pl.BlockSpec((None, tm, d), lambda bi, i: (bi, i, 0))],
        out_shape=[jax.ShapeDtypeStruct((b, t, d), F32), jax.ShapeDtypeStruct((b, t, d), BF16)],
        compiler_params=_params("arbitrary", "arbitrary"),
        name="outproj",
    )(x, merged, w_out_bf, mod, gain2.reshape(1, d))


def _sort_network(n):
    pairs = []

    def merge(lo, size, r):
        step = r * 2
        if step < size:
            merge(lo, size, step)
            merge(lo + r, size, step)
            pairs.extend((i, i + r) for i in range(lo + r, lo + size - r, step))
        else:
            pairs.append((lo, lo + r))

    def sort(lo, size):
        if size > 1:
            sort(lo, size // 2)
            sort(lo + size // 2, size // 2)
            merge(lo, size, 1)

    sort(0, n)
    return pairs


def _exchange(v, i, j):
    v[i], v[j] = jnp.maximum(v[i], v[j]), jnp.minimum(v[i], v[j])


def _top_sorted(groups, k):
    m = len(groups)
    assert m & (m - 1) == 0 and k & (k - 1) == 0
    v = list(groups)
    for i, j in _sort_network(m):
        _exchange(v, i, j)
    v = (v + [jnp.full_like(v[0], -1.0)] * max(k - m, 0))[:k]
    for shift in (4, 2, 1):
        other = [pltpu.roll(x, shift, 0) for x in v]
        v = [jnp.maximum(x, y) for x, y in zip(v, other[::-1])]
        d = k // 2
        while d >= 1:
            for i in range(k):
                if (i // d) % 2 == 0:
                    _exchange(v, i, i + d)
            d //= 2
    return v


def _pack_sublanes(vals, masks):
    counts = [sum(v is u for v in vals) for u in vals]
    base = vals[int(np.argmax(counts))]
    out = base
    for s, v in enumerate(vals):
        if v is not base:
            out = jnp.where(masks[s], v, out)
    return out


def _candidate_slots(topk):
    rows = [[(a, b) for b in range(topk) if (a + 1) * (b + 1) <= topk] for a in range(topk)]
    assert topk == 16
    groups = [rows[0][:8], rows[0][8:], rows[1], rows[2] + rows[4], rows[3] + rows[5] + rows[6],
              rows[7] + [r[0] for r in rows[8:14]], rows[14] + rows[15]]
    assert sorted(p for g in groups for p in g) == sorted(p for r in rows for p in r)
    return [g + [None] * (8 - len(g)) for g in groups]


def _route_kernel(h_ref, wq_ref, k1_ref, k2_ref, e1_ref, e2_ref, th_ref, *, topk):
    nh, nk, half = k1_ref.shape
    tm = h_ref.shape[0]
    qt = lax.dot_general(wq_ref[...], h_ref[...], _NT, preferred_element_type=F32).astype(BF16)
    sub = lax.broadcasted_iota(jnp.int32, (8, tm), 0)
    masks = [sub == s for s in range(8)]
    pad = jnp.full((8, tm), -1.0, F32)
    slots = _candidate_slots(topk)
    for h in range(nh):
        q1 = qt[h * 2 * half:h * 2 * half + half, :]
        q2 = qt[h * 2 * half + half:(h + 1) * 2 * half, :]
        s1 = jnp.dot(k1_ref[h], q1, preferred_element_type=F32)
        s2 = jnp.dot(k2_ref[h], q2, preferred_element_type=F32)
        e1 = jnp.exp(s1 - jnp.max(s1, axis=0, keepdims=True))
        e2 = jnp.exp(s2 - jnp.max(s2, axis=0, keepdims=True))
        g2 = [e2[8 * i:8 * i + 8, :] for i in range(nk // 8)]
        v1 = _top_sorted([e1[8 * i:8 * i + 8, :] for i in range(nk // 8)], topk)
        v2 = _top_sorted(g2, topk)
        a_pk = [_pack_sublanes([pad if p is None else v1[p[0]] for p in g], masks) for g in slots]
        b_pk = [_pack_sublanes([v2[0] if p is None else v2[p[1]] for p in g], masks) for g in slots]
        cand = [a * b for a, b in zip(a_pk, b_pk)]
        best = _top_sorted(cand + [pad], topk)
        z = best[0]
        for w in best[1:]:
            z = z + w
        rz = 1.0 / z
        theta = None
        for a, b, c in zip(a_pk, b_pk, cand):
            t = jnp.where(c >= best[-1], a * (b * rz), 2.0)
            theta = t if theta is None else jnp.minimum(theta, t)
        for shift in (4, 2, 1):
            theta = jnp.minimum(theta, pltpu.roll(theta, shift, 0))
        e1_ref[h] = e1
        e2_ref[h] = jnp.concatenate([g * rz for g in g2], axis=0)
        th_ref[h:h + 1, :] = theta[0:1, :]


def _route(h2, wq_t_bf, k1_bf, k2_bf):
    b, t, d = h2.shape
    nh, nk, half = k1_bf.shape
    qd = wq_t_bf.shape[0]
    tm = _pick(t, (512, 256, 128))
    kern = functools.partial(_route_kernel, topk=PEER_TOPK)
    return pl.pallas_call(
        kern,
        grid=(b, t // tm),
        in_specs=[pl.BlockSpec((None, tm, d), lambda bi, i: (bi, i, 0)),
                  pl.BlockSpec((qd, d), lambda bi, i: (0, 0)),
                  pl.BlockSpec((nh, nk, half), lambda bi, i: (0, 0, 0)),
                  pl.BlockSpec((nh, nk, half), lambda bi, i: (0, 0, 0))],
        out_specs=[pl.BlockSpec((None, nh, nk, tm), lambda bi, i: (bi, 0, 0, i)),
                   pl.BlockSpec((None, nh, nk, tm), lambda bi, i: (bi, 0, 0, i)),
                   pl.BlockSpec((None, nh, tm), lambda bi, i: (bi, 0, i))],
        out_shape=[jax.ShapeDtypeStruct((b, nh, nk, t), F32),
                   jax.ShapeDtypeStruct((b, nh, nk, t), F32),
                   jax.ShapeDtypeStruct((b, nh, t), F32)],
        compiler_params=_params("arbitrary", "arbitrary"),
        name="peer_route",
    )(h2, wq_t_bf, k1_bf, k2_bf)


_GELU_K1 = -2.0 * math.sqrt(2.0 / math.pi) / math.log(2.0)
_GELU_K2 = 0.044715 * _GELU_K1


def _gelu_tanh(x):
    return x / (1.0 + jnp.exp2(x * (_GELU_K1 + _GELU_K2 * (x * x))))


def _peer_kernel(h_ref, dn_ref, up_ref, e1_ref, e2_ref, th_ref, o_ref, p_ref, g_ref, *, lane, n_tiles):
    e = pl.program_id(2)
    nh, nk, tm = e1_ref.shape
    te = dn_ref.shape[0]
    kpt = te // nk
    groups = 8 // kpt

    def build_gates(tile, slot):
        base = pl.multiple_of((tile // groups) * 8, 8)
        sub = tile % groups
        for ii in range(kpt):
            for lc in range(tm // lane):
                ls = slice(lc * lane, (lc + 1) * lane)
                g = jnp.zeros((nk, lane), F32)
                for h in range(nh):
                    grp = e1_ref[h, pl.ds(base, 8), ls]
                    row = grp[ii:ii + 1, :]
                    for s in range(1, groups):
                        row = jnp.where(sub == s, grp[s * kpt + ii:s * kpt + ii + 1, :], row)
                    w = e2_ref[h, :, ls] * row
                    g = g + jnp.where(w >= th_ref[h:h + 1, ls], w, 0.0)
                g_ref[slot, ii * nk:(ii + 1) * nk, ls] = g.astype(g_ref.dtype)

    @pl.when(e == 0)
    def _():
        o_ref[...] = jnp.zeros_like(o_ref)
        build_gates(0, 0)

    build_gates(jnp.minimum(e + 1, n_tiles - 1), (e + 1) % 2)
    act = _gelu_tanh(lax.dot_general(dn_ref[...], h_ref[...], _NT, preferred_element_type=F32))
    p_ref[...] = g_ref[e % 2] * act.astype(p_ref.dtype)
    o_ref[...] += jnp.dot(up_ref[...], p_ref[...], preferred_element_type=F32)


def _peer(h2, e1, e2, th, down_bf, up_t_bf):
    b, t, d = h2.shape
    nh, nk = e1.shape[1], e1.shape[2]
    ne = down_bf.shape[0]
    tm = _pick(t, (1024, 512, 256, 128))
    te = 8 * nk
    assert nk % 8 == 0 and 8 % (te // nk) == 0 and ne == nk * nk
    once = pl.Buffered(1)
    n_tiles = ne // te
    kern = functools.partial(_peer_kernel, lane=128, n_tiles=n_tiles)
    return pl.pallas_call(
        kern,
        grid=(b, t // tm, n_tiles),
        in_specs=[pl.BlockSpec((None, tm, d), lambda bi, i, e: (bi, i, 0), pipeline_mode=once),
                  pl.BlockSpec((te, d), lambda bi, i, e: (e, 0)),
                  pl.BlockSpec((d, te), lambda bi, i, e: (0, e)),
                  pl.BlockSpec((None, nh, nk, tm), lambda bi, i, e: (bi, 0, 0, i), pipeline_mode=once),
                  pl.BlockSpec((None, nh, nk, tm), lambda bi, i, e: (bi, 0, 0, i), pipeline_mode=once),
                  pl.BlockSpec((None, nh, tm), lambda bi, i, e: (bi, 0, i))],
        out_specs=pl.BlockSpec((None, d, tm), lambda bi, i, e: (bi, 0, i), pipeline_mode=once),
        out_shape=jax.ShapeDtypeStruct((b, d, t), F32),
        scratch_shapes=[pltpu.VMEM((te, tm), BF16), pltpu.VMEM((2, te, tm), BF16)],
        compiler_params=_params("arbitrary", "arbitrary", "arbitrary"),
        name="peer_experts",
    )(h2, down_bf, up_t_bf, e1, e2, th)


def _final_kernel(yt_ref, x1_ref, mod_ref, g_ref, o_ref):
    x2 = x1_ref[...] + mod_ref[5:6, :] * yt_ref[...].T
    ms = jnp.mean(x2 * x2, axis=-1, keepdims=True)
    o_ref[...] = x2 * lax.rsqrt(ms + EPS) * g_ref[...]


def _final(y_t, x1, mod, final_gain):
    b, t, d = x1.shape
    tm = _pick(t, (512, 256, 128))
    return pl.pallas_call(
        _final_kernel,
        grid=(b, t // tm),
        in_specs=[pl.BlockSpec((None, d, tm), lambda bi, i: (bi, 0, i)),
                  pl.BlockSpec((None, tm, d), lambda bi, i: (bi, i, 0)),
                  pl.BlockSpec((None, 6, d), lambda bi, i: (bi, 0, 0)),
                  pl.BlockSpec((1, d), lambda bi, i: (0, 0))],
        out_specs=pl.BlockSpec((None, tm, d), lambda bi, i: (bi, i, 0)),
        out_shape=jax.ShapeDtypeStruct((b, t, d), F32),
        compiler_params=_params("arbitrary", "arbitrary"),
        name="final_norm",
    )(y_t, x1, mod, final_gain.reshape(1, d))


def _rope_tables(t, half):
    inv = ROPE_BASE ** (-jnp.arange(half, dtype=F32) / half)
    ang = jnp.arange(t, dtype=F32)[:, None] * inv[None, :]
    return jnp.cos(ang), jnp.sin(ang)


def _trunk(x, mod, w, dims):
    b, t, d = x.shape
    cos, sin = _rope_tables(t, dims["dk"] // 2)
    qkv_v, rqk, rg, gates = _inproj(_hnorm(x, mod, w["norm1_gain"]), cos, sin, w["w_in"], dims)
    na_out = _na(qkv_v, *_na_bias_table(w["na_rel_bias"], t // GRID_W), dims)
    ret_out = _retention(qkv_v, rqk, rg, w["log_gamma"], w["ret_gn_gain"], dims)
    merged = _merge(na_out, ret_out, gates, w["w_na_proj"], w["w_ret_proj"])
    x1, h2 = _outproj(x, merged, w["w_out"], mod, w["norm2_gain"])
    e1, e2, th = _route(h2, w["peer_query_t"], w["peer_keys_1"], w["peer_keys_2"])
    y_t = _peer(h2, e1, e2, th, w["peer_down"], w["peer_up_t"])
    return _final(y_t, x1, mod, w["final_gain"])


def kernel(x_prompt, x_sample, c_prompt, c_sample, ada_w, ada_b, norm1_gain, w_in, na_rel_bias,
           ret_decay_logit, ret_gn_gain, w_na_proj, w_ret_proj, w_out, norm2_gain, peer_query,
           peer_keys_1, peer_keys_2, peer_down, peer_up, final_gain):
    assert ada_w.shape[0] == 1, "the final norm is fused into the single layer's last kernel"
    d = x_prompt.shape[-1]
    na_h = na_rel_bias.shape[1]
    na_w = w_na_proj.shape[1]
    v_w = w_ret_proj.shape[1]
    ret_h = ret_decay_logit.shape[-1]
    qk_w = (w_in.shape[2] - 3 * na_w - 2 * v_w - 2 * d) // 2
    dims = dict(na_h=na_h, na_w=na_w, na_dh=na_w // na_h, ret_h=ret_h, qk_w=qk_w, v_w=v_w,
                dk=qk_w // ret_h, dv=v_w // ret_h, win_h=(na_rel_bias.shape[2] + 1) // 2)
    bp, bs = x_prompt.shape[0], x_sample.shape[0]
    pad = (-(bp + bs)) % 8
    c_all = jnp.concatenate([c_prompt, c_sample, jnp.zeros((pad, d), F32)], axis=0)
    mod = _adaln(c_all, ada_w[0], ada_b[0])
    w = dict(
        norm1_gain=norm1_gain[0], w_in=w_in[0].astype(BF16), na_rel_bias=na_rel_bias[0],
        log_gamma=jax.nn.log_sigmoid(ret_decay_logit[0].astype(F32)), ret_gn_gain=ret_gn_gain[0],
        w_na_proj=w_na_proj[0].astype(BF16), w_ret_proj=w_ret_proj[0].astype(BF16),
        w_out=w_out[0].astype(BF16), norm2_gain=norm2_gain[0],
        peer_query_t=peer_query[0].T.astype(BF16), peer_keys_1=peer_keys_1[0].astype(BF16),
        peer_keys_2=peer_keys_2[0].astype(BF16), peer_down=peer_down[0].astype(BF16),
        peer_up_t=peer_up[0].T.astype(BF16), final_gain=final_gain)
    y_prompt = _trunk(x_prompt, mod[:bp].reshape(bp, 6, d), w, dims)
    y_sample = _trunk(x_sample, mod[bp:bp + bs].reshape(bs, 6, d), w, dims)
    return (y_prompt, y_sample)
```

```python
import functools
import math

import numpy as np
import jax
import jax.numpy as jnp
from jax import lax
from jax.experimental import pallas as pl
from jax.experimental.pallas import tpu as pltpu

GRID_W = 64
PEER_TOPK = 16
ROPE_BASE = 10000.0
EPS = 1e-6
RET_CHUNK = 512
NEG = -1e30

F32 = jnp.float32
BF16 = jnp.bfloat16

VMEM_LIMIT = 56 * 1024 * 1024

_NT = (((1,), (1,)), ((), ()))
_TN = (((0,), (0,)), ((), ()))


def _params(*sem):
    return pltpu.CompilerParams(dimension_semantics=sem, vmem_limit_bytes=VMEM_LIMIT)


def _pick(n, prefs):
    for p in prefs:
        if n % p == 0:
            return p
    return n


def _adaln_kernel(c_ref, w_ref, b_ref, o_ref):
    c = c_ref[...]
    s = c * jax.nn.sigmoid(c)
    o_ref[...] = jnp.dot(s, w_ref[...], preferred_element_type=F32,
                         precision=lax.Precision.HIGHEST) + b_ref[...]


def _adaln(c, ada_w, ada_b):
    m, d = c.shape
    n = ada_w.shape[1]
    tn = _pick(n, (1024, 512, 256, 128))
    return pl.pallas_call(
        _adaln_kernel,
        grid=(n // tn,),
        in_specs=[pl.BlockSpec((m, d), lambda j: (0, 0)),
                  pl.BlockSpec((d, tn), lambda j: (0, j)),
                  pl.BlockSpec((1, tn), lambda j: (0, j))],
        out_specs=pl.BlockSpec((m, tn), lambda j: (0, j)),
        out_shape=jax.ShapeDtypeStruct((m, n), F32),
        compiler_params=_params("arbitrary"),
        name="adaln",
    )(c, ada_w, ada_b.reshape(1, n))


def _modulated_norm(x, gain, shift, scale):
    ms = jnp.mean(x * x, axis=-1, keepdims=True)
    return (x * lax.rsqrt(ms + EPS) * gain) * (1.0 + scale) + shift


def _hnorm_kernel(x_ref, mod_ref, g_ref, o_ref):
    o_ref[...] = _modulated_norm(x_ref[...], g_ref[...], mod_ref[0:1, :], mod_ref[1:2, :]).astype(o_ref.dtype)


def _hnorm(x, mod, gain):
    b, t, d = x.shape
    tm = _pick(t, (512, 256, 128))
    return pl.pallas_call(
        _hnorm_kernel,
        grid=(b, t // tm),
        in_specs=[pl.BlockSpec((None, tm, d), lambda bi, i: (bi, i, 0)),
                  pl.BlockSpec((None, 6, d), lambda bi, i: (bi, 0, 0)),
                  pl.BlockSpec((1, d), lambda bi, i: (0, 0))],
        out_specs=pl.BlockSpec((None, tm, d), lambda bi, i: (bi, i, 0)),
        out_shape=jax.ShapeDtypeStruct((b, t, d), BF16),
        compiler_params=_params("arbitrary", "arbitrary"),
        name="hnorm",
    )(x, mod, gain.reshape(1, d))


def _proj_kernel(*refs, kind, dk):
    h_ref, w_ref = refs[0], refs[1]
    o_ref = refs[-1]
    acc = jnp.dot(h_ref[...], w_ref[...], preferred_element_type=F32)
    if kind == "linear":
        o_ref[...] = (acc * refs[2][...]).astype(o_ref.dtype)
    elif kind == "rotary":
        cs_ref, cos_ref, sin_ref = refs[2:5]
        cos = cos_ref[...]
        sin = sin_ref[...]
        half = dk // 2
        for hh in range(acc.shape[1] // dk):
            lo = slice(hh * dk, hh * dk + half)
            hi = slice(hh * dk + half, (hh + 1) * dk)
            a = acc[:, lo]
            b = acc[:, hi]
            o_ref[:, lo] = ((a * cos - b * sin) * cs_ref[:, lo]).astype(o_ref.dtype)
            o_ref[:, hi] = ((a * sin + b * cos) * cs_ref[:, hi]).astype(o_ref.dtype)
    elif kind == "silu":
        o_ref[...] = (acc * jax.nn.sigmoid(acc)).astype(o_ref.dtype)
    else:
        assert kind == "sigmoid"
        o_ref[...] = jax.nn.sigmoid(acc).astype(o_ref.dtype)


def _proj(h, w_bf, col_block, ncols, tn, kind, extras=(), dk=0):
    b, t, d = h.shape
    tm = _pick(t, (1024, 512, 256, 128))
    extra_specs = []
    for arr in extras:
        if arr.shape[0] == 1:
            extra_specs.append(pl.BlockSpec((1, tn), lambda bi, i, j: (0, j)))
        else:
            extra_specs.append(pl.BlockSpec((tm, arr.shape[1]), lambda bi, i, j: (i, 0)))
    return pl.pallas_call(
        functools.partial(_proj_kernel, kind=kind, dk=dk),
        grid=(b, t // tm, ncols // tn),
        in_specs=[pl.BlockSpec((None, tm, d), lambda bi, i, j: (bi, i, 0)),
                  pl.BlockSpec((d, tn), lambda bi, i, j: (0, col_block(j)))] + extra_specs,
        out_specs=pl.BlockSpec((None, tm, tn), lambda bi, i, j: (bi, i, j)),
        out_shape=jax.ShapeDtypeStruct((b, t, ncols), BF16),
        compiler_params=_params("arbitrary", "arbitrary", "arbitrary"),
        name="inproj_" + kind,
    )(h, w_bf, *extras)


def _inproj(h, cos, sin, w_bf, dims):
    d = h.shape[-1]
    na_w, qk_w, v_w, dk = dims["na_w"], dims["qk_w"], dims["v_w"], dims["dk"]
    tn = 1024
    while any(s % tn for s in (na_w, qk_w, v_w, d)):
        tn //= 2
    assert tn % dk == 0
    n_na, n_qk, n_v = 3 * na_w // tn, 2 * qk_w // tn, v_w // tn
    ones = functools.partial(jnp.ones, dtype=F32)
    cs_lin = jnp.concatenate([jnp.full((na_w,), float(dims["na_dh"]) ** -0.5, F32), ones((2 * na_w + v_w,))])
    cs_rot = jnp.concatenate([ones((qk_w,)), jnp.full((qk_w,), float(dk) ** -0.5, F32)])
    qkv_v = _proj(h, w_bf, lambda j: jnp.where(j < n_na, j, j + n_qk), 3 * na_w + v_w, tn, "linear",
                  (cs_lin.reshape(1, -1),))
    rqk = _proj(h, w_bf, lambda j: n_na + j, 2 * qk_w, tn, "rotary", (cs_rot.reshape(1, -1), cos, sin), dk)
    rg = _proj(h, w_bf, lambda j: n_na + n_qk + n_v + j, v_w, tn, "silu")
    gates = _proj(h, w_bf, lambda j: n_na + n_qk + 2 * n_v + j, 2 * d, tn, "sigmoid")
    return qkv_v, rqk, rg, gates


def _na_plan(rows, win_h):
    kh = min(win_h, rows)
    rpb = _pick(rows, (8, 4, 2, 1))
    kr = min(rows, rpb + kh)
    return kh, rpb, kr


def _na_bias_table(rel_bias, rows):
    nh = rel_bias.shape[0]
    win_h = (rel_bias.shape[1] + 1) // 2
    win_w = (rel_bias.shape[2] + 1) // 2
    kh, rpb, kr = _na_plan(rows, win_h)
    sigs, geoms, ids = [], [], []
    for rb in range(rows // rpb):
        kb = int(np.clip(rb * rpb - (kr - rpb) // 2, 0, rows - kr))
        r = rb * rpb + np.arange(rpb)
        r0 = np.clip(r - kh // 2, 0, rows - kh)
        assert (r0 >= kb).all() and (r0 + kh <= kb + kr).all()
        kabs = kb + np.arange(kr)
        valid = (kabs[None, :] >= r0[:, None]) & (kabs[None, :] < r0[:, None] + kh)
        dr = np.clip(kabs[None, :] - r[:, None] + win_h - 1, 0, 2 * win_h - 2)
        sig = (valid.tobytes(), dr.tobytes())
        if sig not in sigs:
            sigs.append(sig)
            geoms.append((valid, dr))
        ids.append(sigs.index(sig))
    masked = 2 * win_h - 1
    idx = np.stack([np.where(g[0], g[1], masked) for g in geoms])
    assert kr % 2 == 0
    pairs = idx.reshape(len(geoms), rpb, kr // 2, 2)
    combos = sorted({(int(a), int(c)) for a, c in pairs.reshape(-1, 2)})
    pair_ids = np.array([combos.index((int(a), int(c))) for a, c in pairs.reshape(-1, 2)], np.int32)
    cols = np.arange(GRID_W)
    col_start = np.clip(cols - win_w // 2, 0, GRID_W - win_w)
    in_win = (cols[None, :] >= col_start[:, None]) & (cols[None, :] < col_start[:, None] + win_w)
    dc = np.clip(cols[None, :] - cols[:, None] + win_w - 1, 0, 2 * win_w - 2)
    tiles = jnp.where(in_win[None, None], rel_bias.astype(F32)[:, :, dc], NEG)
    tiles = jnp.concatenate([tiles, jnp.full((nh, 1, GRID_W, GRID_W), NEG, F32)], axis=1)
    left = tiles[:, np.array([a for a, _ in combos])]
    right = tiles[:, np.array([c for _, c in combos])]
    tab = jnp.concatenate([left, right], axis=-1)
    return tab, jnp.asarray(ids, jnp.int32), jnp.asarray(pair_ids)


def _na_kernel(vid_ref, pid_ref, q_ref, k_ref, v_ref, b_ref, o_ref, *, rows, rpb, kr, bps):
    step = pl.program_id(2)
    w = GRID_W
    npair = kr // 2
    for u in range(bps):
        rb = step * bps + u
        kb = jnp.clip(rb * rpb - (kr - rpb) // 2, 0, rows - kr)
        start = pl.multiple_of(kb * w, w)
        q = q_ref[u * rpb * w:(u + 1) * rpb * w, :]
        kwin = k_ref[pl.ds(start, kr * w), :]
        vwin = v_ref[pl.ds(start, kr * w), :]
        s = lax.dot_general(q, kwin, _NT, preferred_element_type=F32)
        geom = vid_ref[rb] * (rpb * npair)
        probs, dens = [], []
        for j in range(rpb):
            sj = jnp.concatenate(
                [s[j * w:(j + 1) * w, ip * 2 * w:(ip + 1) * 2 * w] + b_ref[pid_ref[geom + j * npair + ip]]
                 for ip in range(npair)], axis=-1)
            m = jnp.max(sj, axis=-1, keepdims=True)
            p = jnp.exp(sj - m)
            dens.append(jnp.sum(p, axis=-1, keepdims=True))
            probs.append(p.astype(BF16))
        o = jnp.dot(jnp.concatenate(probs, axis=0), vwin, preferred_element_type=F32)
        o_ref[u * rpb * w:(u + 1) * rpb * w, :] = (o / jnp.concatenate(dens, axis=0)).astype(o_ref.dtype)


def _na(proj, bias_tab, block_ids, pair_ids, dims):
    b, t, _ = proj.shape
    nh, dh, na_w = dims["na_h"], dims["na_dh"], dims["na_w"]
    rows = t // GRID_W
    _, rpb, kr = _na_plan(rows, dims["win_h"])
    ncomb = bias_tab.shape[1]
    nblk = rows // rpb
    bps = _pick(nblk, (4, 2, 1))
    hb = na_w // dh
    kern = functools.partial(_na_kernel, rows=rows, rpb=rpb, kr=kr, bps=bps)
    qrows = bps * rpb * GRID_W
    grid_spec = pltpu.PrefetchScalarGridSpec(
        num_scalar_prefetch=2,
        grid=(b, nh, nblk // bps),
        in_specs=[pl.BlockSpec((None, qrows, dh), lambda bi, h, r, vid, pid: (bi, r, h)),
                  pl.BlockSpec((None, t, dh), lambda bi, h, r, vid, pid: (bi, 0, hb + h)),
                  pl.BlockSpec((None, t, dh), lambda bi, h, r, vid, pid: (bi, 0, 2 * hb + h)),
                  pl.BlockSpec((None, ncomb, GRID_W, 2 * GRID_W), lambda bi, h, r, vid, pid: (h, 0, 0, 0))],
        out_specs=pl.BlockSpec((None, qrows, dh), lambda bi, h, r, vid, pid: (bi, r, h)))
    return pl.pallas_call(
        kern,
        grid_spec=grid_spec,
        out_shape=jax.ShapeDtypeStruct((b, t, na_w), BF16),
        compiler_params=_params("arbitrary", "arbitrary", "arbitrary"),
        name="natten",
    )(block_ids, pair_ids, proj, proj, proj, bias_tab)


def _ret_cross(lg, q, k, v, state_ref, qdec_ref, kdec_ref):
    c = q.shape[0]
    qd = (q.astype(F32) * qdec_ref[...]).astype(BF16)
    kd = (k.astype(F32) * kdec_ref[...]).astype(BF16)
    st = state_ref[...]
    out = jnp.dot(qd, st.astype(BF16), preferred_element_type=F32)
    state_ref[...] = st * jnp.exp(c * lg) + lax.dot_general(kd, v, _TN, preferred_element_type=F32)
    return out


def _ret_fwd_kernel(lg_ref, q_ref, k_ref, v_ref, o_ref, state_ref, qdec_ref, kdec_ref, intra_ref):
    hps, c, dk = qdec_ref.shape
    dv = state_ref.shape[-1]
    h0 = pl.program_id(1) * hps

    @pl.when(pl.program_id(2) == 0)
    def _():
        row = lax.broadcasted_iota(jnp.int32, (c, c), 0)
        col = lax.broadcasted_iota(jnp.int32, (c, c), 1)
        pos = lax.broadcasted_iota(jnp.int32, (c, dk), 0).astype(F32)
        for u in range(hps):
            lg, lg_b = lg_ref[0, h0 + u], lg_ref[1, h0 + u]
            intra_ref[u] = jnp.where(row >= col, jnp.exp((row - col).astype(F32) * lg),
                                     jnp.exp((col - row).astype(F32) * lg_b))
            qdec_ref[u] = jnp.exp((pos + 1.0) * lg)
            kdec_ref[u] = jnp.exp((c - 1.0 - pos) * lg)
        state_ref[...] = jnp.zeros_like(state_ref)

    for u in range(hps):
        q = q_ref[:, u * dk:(u + 1) * dk]
        k = k_ref[:, u * dk:(u + 1) * dk]
        v = v_ref[:, u * dv:(u + 1) * dv]
        s = lax.dot_general(q, k, _NT, preferred_element_type=F32) * intra_ref[u]
        o = (jnp.dot(s.astype(BF16), v, preferred_element_type=F32)
             + _ret_cross(lg_ref[0, h0 + u], q, k, v, state_ref.at[u], qdec_ref.at[u], kdec_ref.at[u]))
        o_ref[:, u * dv:(u + 1) * dv] = o.astype(o_ref.dtype)


def _ret_bwd_kernel(lg_ref, q_ref, k_ref, v_ref, f_ref, g_ref, gain_ref, o_ref, state_ref, qdec_ref, kdec_ref):
    hps, c, dk = qdec_ref.shape
    dv = state_ref.shape[-1]
    h0 = pl.program_id(1) * hps

    @pl.when(pl.program_id(2) == 0)
    def _():
        pos = lax.broadcasted_iota(jnp.int32, (c, dk), 0).astype(F32)
        for u in range(hps):
            lg = lg_ref[1, h0 + u]
            qdec_ref[u] = jnp.exp((c - pos) * lg)
            kdec_ref[u] = jnp.exp(pos * lg)
        state_ref[...] = jnp.zeros_like(state_ref)

    for u in range(hps):
        vs = slice(u * dv, (u + 1) * dv)
        q = q_ref[:, u * dk:(u + 1) * dk]
        k = k_ref[:, u * dk:(u + 1) * dk]
        o = (_ret_cross(lg_ref[1, h0 + u], q, k, v_ref[:, vs], state_ref.at[u], qdec_ref.at[u], kdec_ref.at[u])
             + f_ref[:, vs].astype(F32))
        mu = jnp.mean(o, axis=-1, keepdims=True)
        d = o - mu
        var = jnp.mean(d * d, axis=-1, keepdims=True)
        o = d * lax.rsqrt(var + EPS) * gain_ref[:, vs]
        o_ref[:, vs] = (g_ref[:, vs].astype(F32) * o).astype(o_ref.dtype)


def _retention(qkv_v, rqk, rg, log_gamma, gn_gain, dims):
    b, t, _ = rqk.shape
    nh, dk, dv = dims["ret_h"], dims["dk"], dims["dv"]
    na_w, qk_w, v_w = dims["na_w"], dims["qk_w"], dims["v_w"]
    c = _pick(t, (RET_CHUNK, 256, 128))
    nc = t // c
    hps = next(n for n in (4, 2, 1) if nh % n == 0 and (3 * na_w) % (n * dv) == 0)
    scratch = [pltpu.VMEM((hps, dk, dv), F32), pltpu.VMEM((hps, c, dk), F32), pltpu.VMEM((hps, c, dk), F32)]
    kb = nh // hps
    vb = 3 * na_w // (hps * dv)
    smem = pl.BlockSpec(memory_space=pltpu.SMEM)
    fwd = pl.pallas_call(
        _ret_fwd_kernel,
        grid=(b, nh // hps, nc),
        in_specs=[smem,
                  pl.BlockSpec((None, c, hps * dk), lambda bi, h, i: (bi, i, h)),
                  pl.BlockSpec((None, c, hps * dk), lambda bi, h, i: (bi, i, kb + h)),
                  pl.BlockSpec((None, c, hps * dv), lambda bi, h, i: (bi, i, vb + h))],
        out_specs=pl.BlockSpec((None, c, hps * dv), lambda bi, h, i: (bi, i, h)),
        out_shape=jax.ShapeDtypeStruct((b, t, v_w), BF16),
        scratch_shapes=scratch + [pltpu.VMEM((hps, c, c), F32)],
        compiler_params=_params("arbitrary", "arbitrary", "arbitrary"),
        name="ret_fwd",
    )(log_gamma, rqk, rqk, qkv_v)
    return pl.pallas_call(
        _ret_bwd_kernel,
        grid=(b, nh // hps, nc),
        in_specs=[smem,
                  pl.BlockSpec((None, c, hps * dk), lambda bi, h, i: (bi, nc - 1 - i, h)),
                  pl.BlockSpec((None, c, hps * dk), lambda bi, h, i: (bi, nc - 1 - i, kb + h)),
                  pl.BlockSpec((None, c, hps * dv), lambda bi, h, i: (bi, nc - 1 - i, vb + h)),
                  pl.BlockSpec((None, c, hps * dv), lambda bi, h, i: (bi, nc - 1 - i, h)),
                  pl.BlockSpec((None, c, hps * dv), lambda bi, h, i: (bi, nc - 1 - i, h)),
                  pl.BlockSpec((1, hps * dv), lambda bi, h, i: (0, h))],
        out_specs=pl.BlockSpec((None, c, hps * dv), lambda bi, h, i: (bi, nc - 1 - i, h)),
        out_shape=jax.ShapeDtypeStruct((b, t, v_w), BF16),
        scratch_shapes=scratch,
        compiler_params=_params("arbitrary", "arbitrary", "arbitrary"),
        name="ret_bwd",
    )(log_gamma, rqk, rqk, qkv_v, fwd, rg, gn_gain.reshape(1, v_w))


def _merge_kernel(na_ref, ret_ref, wna_ref, wret_ref, gna_ref, gret_ref, o_ref):
    a = jnp.dot(na_ref[...], wna_ref[...], preferred_element_type=F32)
    r = jnp.dot(ret_ref[...], wret_ref[...], preferred_element_type=F32)
    o_ref[...] = (gna_ref[...].astype(F32) * a + gret_ref[...].astype(F32) * r).astype(o_ref.dtype)


def _merge(na_out, ret_out, gates, w_na_bf, w_ret_bf):
    b, t, na_w = na_out.shape
    v_w = ret_out.shape[-1]
    d = w_na_bf.shape[1]
    tm = _pick(t, (1024, 512, 256, 128))
    tn = _pick(d, (512, 256, 128))
    return pl.pallas_call(
        _merge_kernel,
        grid=(b, t // tm, d // tn),
        in_specs=[pl.BlockSpec((None, tm, na_w), lambda bi, i, j: (bi, i, 0)),
                  pl.BlockSpec((None, tm, v_w), lambda bi, i, j: (bi, i, 0)),
                  pl.BlockSpec((na_w, tn), lambda bi, i, j: (0, j)),
                  pl.BlockSpec((v_w, tn), lambda bi, i, j: (0, j)),
                  pl.BlockSpec((None, tm, tn), lambda bi, i, j: (bi, i, j)),
                  pl.BlockSpec((None, tm, tn), lambda bi, i, j: (bi, i, d // tn + j))],
        out_specs=pl.BlockSpec((None, tm, tn), lambda bi, i, j: (bi, i, j)),
        out_shape=jax.ShapeDtypeStruct((b, t, d), BF16),
        compiler_params=_params("arbitrary", "arbitrary", "arbitrary"),
        name="merge",
    )(na_out, ret_out, w_na_bf, w_ret_bf, gates, gates)


def _outproj_kernel(x_ref, m_ref, w_ref, mod_ref, g_ref, x1_ref, h2_ref):
    y = jnp.dot(m_ref[...], w_ref[...], preferred_element_type=F32)
    x1 = x_ref[...] + mod_ref[2:3, :] * y
    x1_ref[...] = x1
    h2_ref[...] = _modulated_norm(x1, g_ref[...], mod_ref[3:4, :], mod_ref[4:5, :]).astype(h2_ref.dtype)


def _outproj(x, merged, w_out_bf, mod, gain2):
    b, t, d = x.shape
    tm = _pick(t, (512, 256, 128))
    return pl.pallas_call(
        _outproj_kernel,
        grid=(b, t // tm),
        in_specs=[pl.BlockSpec((None, tm, d), lambda bi, i: (bi, i, 0)),
                  pl.BlockSpec((None, tm, d), lambda bi, i: (bi, i, 0)),
                  pl.BlockSpec((d, d), lambda bi, i: (0, 0)),
                  pl.BlockSpec((None, 6, d), lambda bi, i: (bi, 0, 0)),
                  pl.BlockSpec((1, d), lambda bi, i: (0, 0))],
        out_specs=[pl.BlockSpec((None, tm, d), lambda bi, i: (bi, i, 0)),
                   pl.BlockSpec((None, tm, d), lambda bi, i: (bi, i, 0))],
        out_shape=[jax.ShapeDtypeStruct((b, t, d), F32), jax.ShapeDtypeStruct((b, t, d), BF16)],
        compiler_params=_params("arbitrary", "arbitrary"),
        name="outproj",
    )(x, merged, w_out_bf, mod, gain2.reshape(1, d))


def _sort_network(n):
    pairs = []

    def merge(lo, size, r):
        step = r * 2
        if step < size:
            merge(lo, size, step)
            merge(lo + r, size, step)
            pairs.extend((i, i + r) for i in range(lo + r, lo + size - r, step))
        else:
            pairs.append((lo, lo + r))

    def sort(lo, size):
        if size > 1:
            sort(lo, size // 2)
            sort(lo + size // 2, size // 2)
            merge(lo, size, 1)

    sort(0, n)
    return pairs


def _exchange(v, i, j):
    v[i], v[j] = jnp.maximum(v[i], v[j]), jnp.minimum(v[i], v[j])


def _top_sorted(groups, k):
    m = len(groups)
    assert m & (m - 1) == 0 and k & (k - 1) == 0
    v = list(groups)
    for i, j in _sort_network(m):
        _exchange(v, i, j)
    v = (v + [jnp.full_like(v[0], -1.0)] * max(k - m, 0))[:k]
    for shift in (4, 2, 1):
        other = [pltpu.roll(x, shift, 0) for x in v]
        v = [jnp.maximum(x, y) for x, y in zip(v, other[::-1])]
        d = k // 2
        while d >= 1:
            for i in range(k):
                if (i // d) % 2 == 0:
                    _exchange(v, i, i + d)
            d //= 2
    return v


def _pack_sublanes(vals, masks):
    counts = [sum(v is u for v in vals) for u in vals]
    base = vals[int(np.argmax(counts))]
    out = base
    for s, v in enumerate(vals):
        if v is not base:
            out = jnp.where(masks[s], v, out)
    return out


def _candidate_slots(topk):
    rows = [[(a, b) for b in range(topk) if (a + 1) * (b + 1) <= topk] for a in range(topk)]
    assert topk == 16
    groups = [rows[0][:8], rows[0][8:], rows[1], rows[2] + rows[4], rows[3] + rows[5] + rows[6],
              rows[7] + [r[0] for r in rows[8:14]], rows[14] + rows[15]]
    assert sorted(p for g in groups for p in g) == sorted(p for r in rows for p in r)
    return [g + [None] * (8 - len(g)) for g in groups]


def _route_kernel(h_ref, wq_ref, k1_ref, k2_ref, e1_ref, e2_ref, th_ref, *, topk):
    nh, nk, half = k1_ref.shape
    tm = h_ref.shape[0]
    qt = lax.dot_general(wq_ref[...], h_ref[...], _NT, preferred_element_type=F32).astype(BF16)
    sub = lax.broadcasted_iota(jnp.int32, (8, tm), 0)
    masks = [sub == s for s in range(8)]
    pad = jnp.full((8, tm), -1.0, F32)
    slots = _candidate_slots(topk)
    for h in range(nh):
        q1 = qt[h * 2 * half:h * 2 * half + half, :]
        q2 = qt[h * 2 * half + half:(h + 1) * 2 * half, :]
        s1 = jnp.dot(k1_ref[h], q1, preferred_element_type=F32)
        s2 = jnp.dot(k2_ref[h], q2, preferred_element_type=F32)
        e1 = jnp.exp(s1 - jnp.max(s1, axis=0, keepdims=True))
        e2 = jnp.exp(s2 - jnp.max(s2, axis=0, keepdims=True))
        g2 = [e2[8 * i:8 * i + 8, :] for i in range(nk // 8)]
        v1 = _top_sorted([e1[8 * i:8 * i + 8, :] for i in range(nk // 8)], topk)
        v2 = _top_sorted(g2, topk)
        a_pk = [_pack_sublanes([pad if p is None else v1[p[0]] for p in g], masks) for g in slots]
        b_pk = [_pack_sublanes([v2[0] if p is None else v2[p[1]] for p in g], masks) for g in slots]
        cand = [a * b for a, b in zip(a_pk, b_pk)]
        best = _top_sorted(cand + [pad], topk)
        z = best[0]
        for w in best[1:]:
            z = z + w
        rz = 1.0 / z
        theta = None
        for a, b, c in zip(a_pk, b_pk, cand):
            t = jnp.where(c >= best[-1], a * (b * rz), 2.0)
            theta = t if theta is None else jnp.minimum(theta, t)
        for shift in (4, 2, 1):
            theta = jnp.minimum(theta, pltpu.roll(theta, shift, 0))
        e1_ref[h] = e1
        e2_ref[h] = jnp.concatenate([g * rz for g in g2], axis=0)
        th_ref[h:h + 1, :] = theta[0:1, :]


def _route(h2, wq_t_bf, k1_bf, k2_bf):
    b, t, d = h2.shape
    nh, nk, half = k1_bf.shape
    qd = wq_t_bf.shape[0]
    tm = _pick(t, (512, 256, 128))
    kern = functools.partial(_route_kernel, topk=PEER_TOPK)
    return pl.pallas_call(
        kern,
        grid=(b, t // tm),
        in_specs=[pl.BlockSpec((None, tm, d), lambda bi, i: (bi, i, 0)),
                  pl.BlockSpec((qd, d), lambda bi, i: (0, 0)),
                  pl.BlockSpec((nh, nk, half), lambda bi, i: (0, 0, 0)),
                  pl.BlockSpec((nh, nk, half), lambda bi, i: (0, 0, 0))],
        out_specs=[pl.BlockSpec((None, nh, nk, tm), lambda bi, i: (bi, 0, 0, i)),
                   pl.BlockSpec((None, nh, nk, tm), lambda bi, i: (bi, 0, 0, i)),
                   pl.BlockSpec((None, nh, tm), lambda bi, i: (bi, 0, i))],
        out_shape=[jax.ShapeDtypeStruct((b, nh, nk, t), F32),
                   jax.ShapeDtypeStruct((b, nh, nk, t), F32),
                   jax.ShapeDtypeStruct((b, nh, t), F32)],
        compiler_params=_params("arbitrary", "arbitrary"),
        name="peer_route",
    )(h2, wq_t_bf, k1_bf, k2_bf)


_GELU_K1 = -2.0 * math.sqrt(2.0 / math.pi) / math.log(2.0)
_GELU_K2 = 0.044715 * _GELU_K1


def _gelu_tanh(x):
    return x / (1.0 + jnp.exp2(x * (_GELU_K1 + _GELU_K2 * (x * x))))


def _peer_kernel(h_ref, dn_ref, up_ref, e1_ref, e2_ref, th_ref, o_ref, p_ref, g_ref, *, lane, n_tiles):
    e = pl.program_id(2)
    nh, nk, tm = e1_ref.shape
    te = dn_ref.shape[0]
    kpt = te // nk
    groups = 8 // kpt

    def build_gates(tile, slot):
        base = pl.multiple_of((tile // groups) * 8, 8)
        sub = tile % groups
        for ii in range(kpt):
            for lc in range(tm // lane):
                ls = slice(lc * lane, (lc + 1) * lane)
                g = jnp.zeros((nk, lane), F32)
                for h in range(nh):
                    grp = e1_ref[h, pl.ds(base, 8), ls]
                    row = grp[ii:ii + 1, :]
                    for s in range(1, groups):
                        row = jnp.where(sub == s, grp[s * kpt + ii:s * kpt + ii + 1, :], row)
                    w = e2_ref[h, :, ls] * row
                    g = g + jnp.where(w >= th_ref[h:h + 1, ls], w, 0.0)
                g_ref[slot, ii * nk:(ii + 1) * nk, ls] = g.astype(g_ref.dtype)

    @pl.when(e == 0)
    def _():
        o_ref[...] = jnp.zeros_like(o_ref)
        build_gates(0, 0)

    build_gates(jnp.minimum(e + 1, n_tiles - 1), (e + 1) % 2)
    act = _gelu_tanh(lax.dot_general(dn_ref[...], h_ref[...], _NT, preferred_element_type=F32))
    p_ref[...] = g_ref[e % 2] * act.astype(p_ref.dtype)
    o_ref[...] += jnp.dot(up_ref[...], p_ref[...], preferred_element_type=F32)


def _peer(h2, e1, e2, th, down_bf, up_t_bf):
    b, t, d = h2.shape
    nh, nk = e1.shape[1], e1.shape[2]
    ne = down_bf.shape[0]
    tm = _pick(t, (1024, 512, 256, 128))
    te = 8 * nk
    assert nk % 8 == 0 and 8 % (te // nk) == 0 and ne == nk * nk
    once = pl.Buffered(1)
    n_tiles = ne // te
    kern = functools.partial(_peer_kernel, lane=128, n_tiles=n_tiles)
    return pl.pallas_call(
        kern,
        grid=(b, t // tm, n_tiles),
        in_specs=[pl.BlockSpec((None, tm, d), lambda bi, i, e: (bi, i, 0), pipeline_mode=once),
                  pl.BlockSpec((te, d), lambda bi, i, e: (e, 0)),
                  pl.BlockSpec((d, te), lambda bi, i, e: (0, e)),
                  pl.BlockSpec((None, nh, nk, tm), lambda bi, i, e: (bi, 0, 0, i), pipeline_mode=once),
                  pl.BlockSpec((None, nh, nk, tm), lambda bi, i, e: (bi, 0, 0, i), pipeline_mode=once),
                  pl.BlockSpec((None, nh, tm), lambda bi, i, e: (bi, 0, i))],
        out_specs=pl.BlockSpec((None, d, tm), lambda bi, i, e: (bi, 0, i), pipeline_mode=once),
        out_shape=jax.ShapeDtypeStruct((b, d, t), F32),
        scratch_shapes=[pltpu.VMEM((te, tm), BF16), pltpu.VMEM((2, te, tm), BF16)],
        compiler_params=_params("arbitrary", "arbitrary", "arbitrary"),
        name="peer_experts",
    )(h2, down_bf, up_t_bf, e1, e2, th)


def _final_kernel(yt_ref, x1_ref, mod_ref, g_ref, o_ref):
    x2 = x1_ref[...] + mod_ref[5:6, :] * yt_ref[...].T
    ms = jnp.mean(x2 * x2, axis=-1, keepdims=True)
    o_ref[...] = x2 * lax.rsqrt(ms + EPS) * g_ref[...]


def _final(y_t, x1, mod, final_gain):
    b, t, d = x1.shape
    tm = _pick(t, (512, 256, 128))
    return pl.pallas_call(
        _final_kernel,
        grid=(b, t // tm),
        in_specs=[pl.BlockSpec((None, d, tm), lambda bi, i: (bi, 0, i)),
                  pl.BlockSpec((None, tm, d), lambda bi, i: (bi, i, 0)),
                  pl.BlockSpec((None, 6, d), lambda bi, i: (bi, 0, 0)),
                  pl.BlockSpec((1, d), lambda bi, i: (0, 0))],
        out_specs=pl.BlockSpec((None, tm, d), lambda bi, i: (bi, i, 0)),
        out_shape=jax.ShapeDtypeStruct((b, t, d), F32),
        compiler_params=_params("arbitrary", "arbitrary"),
        name="final_norm",
    )(y_t, x1, mod, final_gain.reshape(1, d))


def _rope_tables(t, half):
    inv = ROPE_BASE ** (-jnp.arange(half, dtype=F32) / half)
    ang = jnp.arange(t, dtype=F32)[:, None] * inv[None, :]
    return jnp.cos(ang), jnp.sin(ang)


def _trunk(x, mod, w, dims):
    b, t, d = x.shape
    cos, sin = _rope_tables(t, dims["dk"] // 2)
    qkv_v, rqk, rg, gates = _inproj(_hnorm(x, mod, w["norm1_gain"]), cos, sin, w["w_in"], dims)
    na_out = _na(qkv_v, *_na_bias_table(w["na_rel_bias"], t // GRID_W), dims)
    ret_out = _retention(qkv_v, rqk, rg, w["log_gamma"], w["ret_gn_gain"], dims)
    merged = _merge(na_out, ret_out, gates, w["w_na_proj"], w["w_ret_proj"])
    x1, h2 = _outproj(x, merged, w["w_out"], mod, w["norm2_gain"])
    e1, e2, th = _route(h2, w["peer_query_t"], w["peer_keys_1"], w["peer_keys_2"])
    y_t = _peer(h2, e1, e2, th, w["peer_down"], w["peer_up_t"])
    return _final(y_t, x1, mod, w["final_gain"])


def kernel(x_prompt, x_sample, c_prompt, c_sample, ada_w, ada_b, norm1_gain, w_in, na_rel_bias,
           ret_decay_logit, ret_gn_gain, w_na_proj, w_ret_proj, w_out, norm2_gain, peer_query,
           peer_keys_1, peer_keys_2, peer_down, peer_up, final_gain):
    assert ada_w.shape[0] == 1, "the final norm is fused into the single layer's last kernel"
    d = x_prompt.shape[-1]
    na_h = na_rel_bias.shape[1]
    na_w = w_na_proj.shape[1]
    v_w = w_ret_proj.shape[1]
    ret_h = ret_decay_logit.shape[-1]
    qk_w = (w_in.shape[2] - 3 * na_w - 2 * v_w - 2 * d) // 2
    dims = dict(na_h=na_h, na_w=na_w, na_dh=na_w // na_h, ret_h=ret_h, qk_w=qk_w, v_w=v_w,
                dk=qk_w // ret_h, dv=v_w // ret_h, win_h=(na_rel_bias.shape[2] + 1) // 2)
    bp, bs = x_prompt.shape[0], x_sample.shape[0]
    pad = (-(bp + bs)) % 8
    c_all = jnp.concatenate([c_prompt, c_sample, jnp.zeros((pad, d), F32)], axis=0)
    mod = _adaln(c_all, ada_w[0], ada_b[0])
    w = dict(
        norm1_gain=norm1_gain[0], w_in=w_in[0].astype(BF16), na_rel_bias=na_rel_bias[0],
        log_gamma=jax.nn.log_sigmoid(ret_decay_logit[0].astype(F32)), ret_gn_gain=ret_gn_gain[0],
        w_na_proj=w_na_proj[0].astype(BF16), w_ret_proj=w_ret_proj[0].astype(BF16),
        w_out=w_out[0].astype(BF16), norm2_gain=norm2_gain[0],
        peer_query_t=peer_query[0].T.astype(BF16), peer_keys_1=peer_keys_1[0].astype(BF16),
        peer_keys_2=peer_keys_2[0].astype(BF16), peer_down=peer_down[0].astype(BF16),
        peer_up_t=peer_up[0].T.astype(BF16), final_gain=final_gain)
    y_prompt = _trunk(x_prompt, mod[:bp].reshape(bp, 6, d), w, dims)
    y_sample = _trunk(x_sample, mod[bp:bp + bs].reshape(bs, 6, d), w, dims)
    return (y_prompt, y_sample)
```

```python
import functools
import math

import numpy as np
import jax
import jax.numpy as jnp
from jax import lax
from jax.experimental import pallas as pl
from jax.experimental.pallas import tpu as pltpu

GRID_W = 64
PEER_TOPK = 16
ROPE_BASE = 10000.0
EPS = 1e-6
RET_CHUNK = 512
NEG = -1e30

F32 = jnp.float32
BF16 = jnp.bfloat16

VMEM_LIMIT = 56 * 1024 * 1024

_NT = (((1,), (1,)), ((), ()))
_TN = (((0,), (0,)), ((), ()))


def _params(*sem):
    return pltpu.CompilerParams(dimension_semantics=sem, vmem_limit_bytes=VMEM_LIMIT)


def _pick(n, prefs):
    for p in prefs:
        if n % p == 0:
            return p
    return n


def _adaln_kernel(c_ref, w_ref, b_ref, o_ref):
    c = c_ref[...]
    s = c * jax.nn.sigmoid(c)
    o_ref[...] = jnp.dot(s, w_ref[...], preferred_element_type=F32,
                         precision=lax.Precision.HIGHEST) + b_ref[...]


def _adaln(c, ada_w, ada_b):
    m, d = c.shape
    n = ada_w.shape[1]
    tn = _pick(n, (1024, 512, 256, 128))
    return pl.pallas_call(
        _adaln_kernel,
        grid=(n // tn,),
        in_specs=[pl.BlockSpec((m, d), lambda j: (0, 0)),
                  pl.BlockSpec((d, tn), lambda j: (0, j)),
                  pl.BlockSpec((1, tn), lambda j: (0, j))],
        out_specs=pl.BlockSpec((m, tn), lambda j: (0, j)),
        out_shape=jax.ShapeDtypeStruct((m, n), F32),
        compiler_params=_params("arbitrary"),
        name="adaln",
    )(c, ada_w, ada_b.reshape(1, n))


def _modulated_norm(x, gain, shift, scale):
    ms = jnp.mean(x * x, axis=-1, keepdims=True)
    return (x * lax.rsqrt(ms + EPS) * gain) * (1.0 + scale) + shift


def _hnorm_kernel(x_ref, mod_ref, g_ref, o_ref):
    o_ref[...] = _modulated_norm(x_ref[...], g_ref[...], mod_ref[0:1, :], mod_ref[1:2, :]).astype(o_ref.dtype)


def _hnorm(x, mod, gain):
    b, t, d = x.shape
    tm = _pick(t, (512, 256, 128))
    return pl.pallas_call(
        _hnorm_kernel,
        grid=(b, t // tm),
        in_specs=[pl.BlockSpec((None, tm, d), lambda bi, i: (bi, i, 0)),
                  pl.BlockSpec((None, 6, d), lambda bi, i: (bi, 0, 0)),
                  pl.BlockSpec((1, d), lambda bi, i: (0, 0))],
        out_specs=pl.BlockSpec((None, tm, d), lambda bi, i: (bi, i, 0)),
        out_shape=jax.ShapeDtypeStruct((b, t, d), BF16),
        compiler_params=_params("arbitrary", "arbitrary"),
        name="hnorm",
    )(x, mod, gain.reshape(1, d))


def _proj_kernel(*refs, kind, dk):
    h_ref, w_ref = refs[0], refs[1]
    o_ref = refs[-1]
    acc = jnp.dot(h_ref[...], w_ref[...], preferred_element_type=F32)
    if kind == "linear":
        o_ref[...] = (acc * refs[2][...]).astype(o_ref.dtype)
    elif kind == "rotary":
        cs_ref, cos_ref, sin_ref = refs[2:5]
        cos = cos_ref[...]
        sin = sin_ref[...]
        half = dk // 2
        for hh in range(acc.shape[1] // dk):
            lo = slice(hh * dk, hh * dk + half)
            hi = slice(hh * dk + half, (hh + 1) * dk)
            a = acc[:, lo]
            b = acc[:, hi]
            o_ref[:, lo] = ((a * cos - b * sin) * cs_ref[:, lo]).astype(o_ref.dtype)
            o_ref[:, hi] = ((a * sin + b * cos) * cs_ref[:, hi]).astype(o_ref.dtype)
    elif kind == "silu":
        o_ref[...] = (acc * jax.nn.sigmoid(acc)).astype(o_ref.dtype)
    else:
        assert kind == "sigmoid"
        o_ref[...] = jax.nn.sigmoid(acc).astype(o_ref.dtype)


def _proj(h, w_bf, col_block, ncols, tn, kind, extras=(), dk=0):
    b, t, d = h.shape
    tm = _pick(t, (1024, 512, 256, 128))
    extra_specs = []
    for arr in extras:
        if arr.shape[0] == 1:
            extra_specs.append(pl.BlockSpec((1, tn), lambda bi, i, j: (0, j)))
        else:
            extra_specs.append(pl.BlockSpec((tm, arr.shape[1]), lambda bi, i, j: (i, 0)))
    return pl.pallas_call(
        functools.partial(_proj_kernel, kind=kind, dk=dk),
        grid=(b, t // tm, ncols // tn),
        in_specs=[pl.BlockSpec((None, tm, d), lambda bi, i, j: (bi, i, 0)),
                  pl.BlockSpec((d, tn), lambda bi, i, j: (0, col_block(j)))] + extra_specs,
        out_specs=pl.BlockSpec((None, tm, tn), lambda bi, i, j: (bi, i, j)),
        out_shape=jax.ShapeDtypeStruct((b, t, ncols), BF16),
        compiler_params=_params("arbitrary", "arbitrary", "arbitrary"),
        name="inproj_" + kind,
    )(h, w_bf, *extras)


def _inproj(h, cos, sin, w_bf, dims):
    d = h.shape[-1]
    na_w, qk_w, v_w, dk = dims["na_w"], dims["qk_w"], dims["v_w"], dims["dk"]
    tn = 2048
    while any(s % tn for s in (na_w, qk_w, v_w, d)):
        tn //= 2
    assert tn % dk == 0
    n_na, n_qk, n_v = 3 * na_w // tn, 2 * qk_w // tn, v_w // tn
    ones = functools.partial(jnp.ones, dtype=F32)
    cs_lin = jnp.concatenate([jnp.full((na_w,), float(dims["na_dh"]) ** -0.5, F32), ones((2 * na_w + v_w,))])
    cs_rot = jnp.concatenate([ones((qk_w,)), jnp.full((qk_w,), float(dk) ** -0.5, F32)])
    qkv_v = _proj(h, w_bf, lambda j: jnp.where(j < n_na, j, j + n_qk), 3 * na_w + v_w, tn, "linear",
                  (cs_lin.reshape(1, -1),))
    rqk = _proj(h, w_bf, lambda j: n_na + j, 2 * qk_w, tn, "rotary", (cs_rot.reshape(1, -1), cos, sin), dk)
    rg = _proj(h, w_bf, lambda j: n_na + n_qk + n_v + j, v_w, tn, "silu")
    gates = _proj(h, w_bf, lambda j: n_na + n_qk + 2 * n_v + j, 2 * d, tn, "sigmoid")
    return qkv_v, rqk, rg, gates


def _na_plan(rows, win_h):
    kh = min(win_h, rows)
    rpb = _pick(rows, (8, 4, 2, 1))
    kr = min(rows, rpb + kh)
    return kh, rpb, kr


def _na_bias_table(rel_bias, rows):
    nh = rel_bias.shape[0]
    win_h = (rel_bias.shape[1] + 1) // 2
    win_w = (rel_bias.shape[2] + 1) // 2
    kh, rpb, kr = _na_plan(rows, win_h)
    sigs, geoms, ids = [], [], []
    for rb in range(rows // rpb):
        kb = int(np.clip(rb * rpb - (kr - rpb) // 2, 0, rows - kr))
        r = rb * rpb + np.arange(rpb)
        r0 = np.clip(r - kh // 2, 0, rows - kh)
        assert (r0 >= kb).all() and (r0 + kh <= kb + kr).all()
        kabs = kb + np.arange(kr)
        valid = (kabs[None, :] >= r0[:, None]) & (kabs[None, :] < r0[:, None] + kh)
        dr = np.clip(kabs[None, :] - r[:, None] + win_h - 1, 0, 2 * win_h - 2)
        sig = (valid.tobytes(), dr.tobytes())
        if sig not in sigs:
            sigs.append(sig)
            geoms.append((valid, dr))
        ids.append(sigs.index(sig))
    masked = 2 * win_h - 1
    idx = np.stack([np.where(g[0], g[1], masked) for g in geoms])
    assert kr % 2 == 0
    pairs = idx.reshape(len(geoms), rpb, kr // 2, 2)
    combos = sorted({(int(a), int(c)) for a, c in pairs.reshape(-1, 2)})
    pair_ids = np.array([combos.index((int(a), int(c))) for a, c in pairs.reshape(-1, 2)], np.int32)
    cols = np.arange(GRID_W)
    col_start = np.clip(cols - win_w // 2, 0, GRID_W - win_w)
    in_win = (cols[None, :] >= col_start[:, None]) & (cols[None, :] < col_start[:, None] + win_w)
    dc = np.clip(cols[None, :] - cols[:, None] + win_w - 1, 0, 2 * win_w - 2)
    tiles = jnp.where(in_win[None, None], rel_bias.astype(F32)[:, :, dc], NEG)
    tiles = jnp.concatenate([tiles, jnp.full((nh, 1, GRID_W, GRID_W), NEG, F32)], axis=1)
    left = tiles[:, np.array([a for a, _ in combos])]
    right = tiles[:, np.array([c for _, c in combos])]
    tab = jnp.concatenate([left, right], axis=-1)
    return tab, jnp.asarray(ids, jnp.int32), jnp.asarray(pair_ids)


def _na_kernel(vid_ref, pid_ref, q_ref, k_ref, v_ref, b_ref, o_ref, *, rows, rpb, kr, bps):
    step = pl.program_id(2)
    w = GRID_W
    npair = kr // 2
    for u in range(bps):
        rb = step * bps + u
        kb = jnp.clip(rb * rpb - (kr - rpb) // 2, 0, rows - kr)
        start = pl.multiple_of(kb * w, w)
        q = q_ref[u * rpb * w:(u + 1) * rpb * w, :]
        kwin = k_ref[pl.ds(start, kr * w), :]
        vwin = v_ref[pl.ds(start, kr * w), :]
        s = lax.dot_general(q, kwin, _NT, preferred_element_type=F32)
        geom = vid_ref[rb] * (rpb * npair)
        probs, dens = [], []
        for j in range(rpb):
            sj = jnp.concatenate(
                [s[j * w:(j + 1) * w, ip * 2 * w:(ip + 1) * 2 * w] + b_ref[pid_ref[geom + j * npair + ip]]
                 for ip in range(npair)], axis=-1)
            m = jnp.max(sj, axis=-1, keepdims=True)
            p = jnp.exp(sj - m)
            dens.append(jnp.sum(p, axis=-1, keepdims=True))
            probs.append(p.astype(BF16))
        o = jnp.dot(jnp.concatenate(probs, axis=0), vwin, preferred_element_type=F32)
        o_ref[u * rpb * w:(u + 1) * rpb * w, :] = (o / jnp.concatenate(dens, axis=0)).astype(o_ref.dtype)


def _na(proj, bias_tab, block_ids, pair_ids, dims):
    b, t, _ = proj.shape
    nh, dh, na_w = dims["na_h"], dims["na_dh"], dims["na_w"]
    rows = t // GRID_W
    _, rpb, kr = _na_plan(rows, dims["win_h"])
    ncomb = bias_tab.shape[1]
    nblk = rows // rpb
    bps = _pick(nblk, (16, 8, 4, 2, 1))
    hb = na_w // dh
    kern = functools.partial(_na_kernel, rows=rows, rpb=rpb, kr=kr, bps=bps)
    qrows = bps * rpb * GRID_W
    grid_spec = pltpu.PrefetchScalarGridSpec(
        num_scalar_prefetch=2,
        grid=(b, nh, nblk // bps),
        in_specs=[pl.BlockSpec((None, qrows, dh), lambda bi, h, r, vid, pid: (bi, r, h)),
                  pl.BlockSpec((None, t, dh), lambda bi, h, r, vid, pid: (bi, 0, hb + h)),
                  pl.BlockSpec((None, t, dh), lambda bi, h, r, vid, pid: (bi, 0, 2 * hb + h)),
                  pl.BlockSpec((None, ncomb, GRID_W, 2 * GRID_W), lambda bi, h, r, vid, pid: (h, 0, 0, 0))],
        out_specs=pl.BlockSpec((None, qrows, dh), lambda bi, h, r, vid, pid: (bi, r, h)))
    return pl.pallas_call(
        kern,
        grid_spec=grid_spec,
        out_shape=jax.ShapeDtypeStruct((b, t, na_w), BF16),
        compiler_params=_params("arbitrary", "arbitrary", "arbitrary"),
        name="natten",
    )(block_ids, pair_ids, proj, proj, proj, bias_tab)


def _ret_cross(lg, q, k, v, state_ref, qdec_ref, kdec_ref):
    c = q.shape[0]
    qd = (q.astype(F32) * qdec_ref[...]).astype(BF16)
    kd = (k.astype(F32) * kdec_ref[...]).astype(BF16)
    st = state_ref[...]
    out = jnp.dot(qd, st.astype(BF16), preferred_element_type=F32)
    state_ref[...] = st * jnp.exp(c * lg) + lax.dot_general(kd, v, _TN, preferred_element_type=F32)
    return out


def _ret_fwd_kernel(lg_ref, q_ref, k_ref, v_ref, o_ref, state_ref, qdec_ref, kdec_ref, intra_ref):
    hps, c, dk = qdec_ref.shape
    dv = state_ref.shape[-1]
    h0 = pl.program_id(1) * hps

    @pl.when(pl.program_id(2) == 0)
    def _():
        row = lax.broadcasted_iota(jnp.int32, (c, c), 0)
        col = lax.broadcasted_iota(jnp.int32, (c, c), 1)
        pos = lax.broadcasted_iota(jnp.int32, (c, dk), 0).astype(F32)
        for u in range(hps):
            lg, lg_b = lg_ref[0, h0 + u], lg_ref[1, h0 + u]
            intra_ref[u] = jnp.where(row >= col, jnp.exp((row - col).astype(F32) * lg),
                                     jnp.exp((col - row).astype(F32) * lg_b))
            qdec_ref[u] = jnp.exp((pos + 1.0) * lg)
            kdec_ref[u] = jnp.exp((c - 1.0 - pos) * lg)
        state_ref[...] = jnp.zeros_like(state_ref)

    for u in range(hps):
        q = q_ref[:, u * dk:(u + 1) * dk]
        k = k_ref[:, u * dk:(u + 1) * dk]
        v = v_ref[:, u * dv:(u + 1) * dv]
        s = lax.dot_general(q, k, _NT, preferred_element_type=F32) * intra_ref[u]
        o = (jnp.dot(s.astype(BF16), v, preferred_element_type=F32)
             + _ret_cross(lg_ref[0, h0 + u], q, k, v, state_ref.at[u], qdec_ref.at[u], kdec_ref.at[u]))
        o_ref[:, u * dv:(u + 1) * dv] = o.astype(o_ref.dtype)


def _ret_bwd_kernel(lg_ref, q_ref, k_ref, v_ref, f_ref, g_ref, gain_ref, o_ref, state_ref, qdec_ref, kdec_ref):
    hps, c, dk = qdec_ref.shape
    dv = state_ref.shape[-1]
    h0 = pl.program_id(1) * hps

    @pl.when(pl.program_id(2) == 0)
    def _():
        pos = lax.broadcasted_iota(jnp.int32, (c, dk), 0).astype(F32)
        for u in range(hps):
            lg = lg_ref[1, h0 + u]
            qdec_ref[u] = jnp.exp((c - pos) * lg)
            kdec_ref[u] = jnp.exp(pos * lg)
        state_ref[...] = jnp.zeros_like(state_ref)

    for u in range(hps):
        vs = slice(u * dv, (u + 1) * dv)
        q = q_ref[:, u * dk:(u + 1) * dk]
        k = k_ref[:, u * dk:(u + 1) * dk]
        o = (_ret_cross(lg_ref[1, h0 + u], q, k, v_ref[:, vs], state_ref.at[u], qdec_ref.at[u], kdec_ref.at[u])
             + f_ref[:, vs].astype(F32))
        mu = jnp.mean(o, axis=-1, keepdims=True)
        d = o - mu
        var = jnp.mean(d * d, axis=-1, keepdims=True)
        o = d * lax.rsqrt(var + EPS) * gain_ref[:, vs]
        o_ref[:, vs] = (g_ref[:, vs].astype(F32) * o).astype(o_ref.dtype)


def _retention(qkv_v, rqk, rg, log_gamma, gn_gain, dims):
    b, t, _ = rqk.shape
    nh, dk, dv = dims["ret_h"], dims["dk"], dims["dv"]
    na_w, qk_w, v_w = dims["na_w"], dims["qk_w"], dims["v_w"]
    c = _pick(t, (RET_CHUNK, 256, 128))
    nc = t // c
    hps = next(n for n in (4, 2, 1) if nh % n == 0 and (3 * na_w) % (n * dv) == 0)
    scratch = [pltpu.VMEM((hps, dk, dv), F32), pltpu.VMEM((hps, c, dk), F32), pltpu.VMEM((hps, c, dk), F32)]
    kb = nh // hps
    vb = 3 * na_w // (hps * dv)
    smem = pl.BlockSpec(memory_space=pltpu.SMEM)
    fwd = pl.pallas_call(
        _ret_fwd_kernel,
        grid=(b, nh // hps, nc),
        in_specs=[smem,
                  pl.BlockSpec((None, c, hps * dk), lambda bi, h, i: (bi, i, h)),
                  pl.BlockSpec((None, c, hps * dk), lambda bi, h, i: (bi, i, kb + h)),
                  pl.BlockSpec((None, c, hps * dv), lambda bi, h, i: (bi, i, vb + h))],
        out_specs=pl.BlockSpec((None, c, hps * dv), lambda bi, h, i: (bi, i, h)),
        out_shape=jax.ShapeDtypeStruct((b, t, v_w), BF16),
        scratch_shapes=scratch + [pltpu.VMEM((hps, c, c), F32)],
        compiler_params=_params("arbitrary", "arbitrary", "arbitrary"),
        name="ret_fwd",
    )(log_gamma, rqk, rqk, qkv_v)
    return pl.pallas_call(
        _ret_bwd_kernel,
        grid=(b, nh // hps, nc),
        in_specs=[smem,
                  pl.BlockSpec((None, c, hps * dk), lambda bi, h, i: (bi, nc - 1 - i, h)),
                  pl.BlockSpec((None, c, hps * dk), lambda bi, h, i: (bi, nc - 1 - i, kb + h)),
                  pl.BlockSpec((None, c, hps * dv), lambda bi, h, i: (bi, nc - 1 - i, vb + h)),
                  pl.BlockSpec((None, c, hps * dv), lambda bi, h, i: (bi, nc - 1 - i, h)),
                  pl.BlockSpec((None, c, hps * dv), lambda bi, h, i: (bi, nc - 1 - i, h)),
                  pl.BlockSpec((1, hps * dv), lambda bi, h, i: (0, h))],
        out_specs=pl.BlockSpec((None, c, hps * dv), lambda bi, h, i: (bi, nc - 1 - i, h)),
        out_shape=jax.ShapeDtypeStruct((b, t, v_w), BF16),
        scratch_shapes=scratch,
        compiler_params=_params("arbitrary", "arbitrary", "arbitrary"),
        name="ret_bwd",
    )(log_gamma, rqk, rqk, qkv_v, fwd, rg, gn_gain.reshape(1, v_w))


def _merge_kernel(na_ref, ret_ref, wna_ref, wret_ref, gna_ref, gret_ref, o_ref):
    a = jnp.dot(na_ref[...], wna_ref[...], preferred_element_type=F32)
    r = jnp.dot(ret_ref[...], wret_ref[...], preferred_element_type=F32)
    o_ref[...] = (gna_ref[...].astype(F32) * a + gret_ref[...].astype(F32) * r).astype(o_ref.dtype)


def _merge(na_out, ret_out, gates, w_na_bf, w_ret_bf):
    b, t, na_w = na_out.shape
    v_w = ret_out.shape[-1]
    d = w_na_bf.shape[1]
    tm = _pick(t, (1024, 512, 256, 128))
    tn = _pick(d, (512, 256, 128))
    return pl.pallas_call(
        _merge_kernel,
        grid=(b, t // tm, d // tn),
        in_specs=[pl.BlockSpec((None, tm, na_w), lambda bi, i, j: (bi, i, 0)),
                  pl.BlockSpec((None, tm, v_w), lambda bi, i, j: (bi, i, 0)),
                  pl.BlockSpec((na_w, tn), lambda bi, i, j: (0, j)),
                  pl.BlockSpec((v_w, tn), lambda bi, i, j: (0, j)),
                  pl.BlockSpec((None, tm, tn), lambda bi, i, j: (bi, i, j)),
                  pl.BlockSpec((None, tm, tn), lambda bi, i, j: (bi, i, d // tn + j))],
        out_specs=pl.BlockSpec((None, tm, tn), lambda bi, i, j: (bi, i, j)),
        out_shape=jax.ShapeDtypeStruct((b, t, d), BF16),
        compiler_params=_params("arbitrary", "arbitrary", "arbitrary"),
        name="merge",
    )(na_out, ret_out, w_na_bf, w_ret_bf, gates, gates)


def _outproj_kernel(x_ref, m_ref, w_ref, mod_ref, g_ref, x1_ref, h2_ref):
    y = jnp.dot(m_ref[...], w_ref[...], preferred_element_type=F32)
    x1 = x_ref[...] + mod_ref[2:3, :] * y
    x1_ref[...] = x1
    h2_ref[...] = _modulated_norm(x1, g_ref[...], mod_ref[3:4, :], mod_ref[4:5, :]).astype(h2_ref.dtype)


def _outproj(x, merged, w_out_bf, mod, gain2):
    b, t, d = x.shape
    tm = _pick(t, (512, 256, 128))
    return pl.pallas_call(
        _outproj_kernel,
        grid=(b, t // tm),
        in_specs=[pl.BlockSpec((None, tm, d), lambda bi, i: (bi, i, 0)),
                  pl.BlockSpec((None, tm, d), lambda bi, i: (bi, i, 0)),
                  pl.BlockSpec((d, d), lambda bi, i: (0, 0)),
                  pl.BlockSpec((None, 6, d), lambda bi, i: (bi, 0, 0)),
                  pl.BlockSpec((1, d), lambda bi, i: (0, 0))],
        out_specs=[pl.BlockSpec((None, tm, d), lambda bi, i: (bi, i, 0)),
                   pl.BlockSpec((None, tm, d), lambda bi, i: (bi, i, 0))],
        out_shape=[jax.ShapeDtypeStruct((b, t, d), F32), jax.ShapeDtypeStruct((b, t, d), BF16)],
        compiler_params=_params("arbitrary", "arbitrary"),
        name="outproj",
    )(x, merged, w_out_bf, mod, gain2.reshape(1, d))


def _sort_network(n):
    pairs = []

    def merge(lo, size, r):
        step = r * 2
        if step < size:
            merge(lo, size, step)
            merge(lo + r, size, step)
            pairs.extend((i, i + r) for i in range(lo + r, lo + size - r, step))
        else:
            pairs.append((lo, lo + r))

    def sort(lo, size):
        if size > 1:
            sort(lo, size // 2)
            sort(lo + size // 2, size // 2)
            merge(lo, size, 1)

    sort(0, n)
    return pairs


def _exchange(v, i, j):
    v[i], v[j] = jnp.maximum(v[i], v[j]), jnp.minimum(v[i], v[j])


def _top_sorted(groups, k):
    m = len(groups)
    assert m & (m - 1) == 0 and k & (k - 1) == 0
    v = list(groups)
    for i, j in _sort_network(m):
        _exchange(v, i, j)
    v = (v + [jnp.full_like(v[0], -1.0)] * max(k - m, 0))[:k]
    for shift in (4, 2, 1):
        other = [pltpu.roll(x, shift, 0) for x in v]
        v = [jnp.maximum(x, y) for x, y in zip(v, other[::-1])]
        d = k // 2
        while d >= 1:
            for i in range(k):
                if (i // d) % 2 == 0:
                    _exchange(v, i, i + d)
            d //= 2
    return v


def _pack_sublanes(vals, masks):
    counts = [sum(v is u for v in vals) for u in vals]
    base = vals[int(np.argmax(counts))]
    out = base
    for s, v in enumerate(vals):
        if v is not base:
            out = jnp.where(masks[s], v, out)
    return out


def _candidate_slots(topk):
    rows = [[(a, b) for b in range(topk) if (a + 1) * (b + 1) <= topk] for a in range(topk)]
    assert topk == 16
    groups = [rows[0][:8], rows[0][8:], rows[1], rows[2] + rows[4], rows[3] + rows[5] + rows[6],
              rows[7] + [r[0] for r in rows[8:14]], rows[14] + rows[15]]
    assert sorted(p for g in groups for p in g) == sorted(p for r in rows for p in r)
    return [g + [None] * (8 - len(g)) for g in groups]


def _route_kernel(h_ref, wq_ref, k1_ref, k2_ref, e1_ref, e2_ref, th_ref, *, topk):
    nh, nk, half = k1_ref.shape
    tm = h_ref.shape[0]
    qt = lax.dot_general(wq_ref[...], h_ref[...], _NT, preferred_element_type=F32).astype(BF16)
    sub = lax.broadcasted_iota(jnp.int32, (8, tm), 0)
    masks = [sub == s for s in range(8)]
    pad = jnp.full((8, tm), -1.0, F32)
    slots = _candidate_slots(topk)
    for h in range(nh):
        q1 = qt[h * 2 * half:h * 2 * half + half, :]
        q2 = qt[h * 2 * half + half:(h + 1) * 2 * half, :]
        s1 = jnp.dot(k1_ref[h], q1, preferred_element_type=F32)
        s2 = jnp.dot(k2_ref[h], q2, preferred_element_type=F32)
        e1 = jnp.exp(s1 - jnp.max(s1, axis=0, keepdims=True))
        e2 = jnp.exp(s2 - jnp.max(s2, axis=0, keepdims=True))
        g2 = [e2[8 * i:8 * i + 8, :] for i in range(nk // 8)]
        v1 = _top_sorted([e1[8 * i:8 * i + 8, :] for i in range(nk // 8)], topk)
        v2 = _top_sorted(g2, topk)
        a_pk = [_pack_sublanes([pad if p is None else v1[p[0]] for p in g], masks) for g in slots]
        b_pk = [_pack_sublanes([v2[0] if p is None else v2[p[1]] for p in g], masks) for g in slots]
        cand = [a * b for a, b in zip(a_pk, b_pk)]
        best = _top_sorted(cand + [pad], topk)
        z = best[0]
        for w in best[1:]:
            z = z + w
        rz = 1.0 / z
        theta = None
        for a, b, c in zip(a_pk, b_pk, cand):
            t = jnp.where(c >= best[-1], a * (b * rz), 2.0)
            theta = t if theta is None else jnp.minimum(theta, t)
        for shift in (4, 2, 1):
            theta = jnp.minimum(theta, pltpu.roll(theta, shift, 0))
        e1_ref[h] = e1
        e2_ref[h] = jnp.concatenate([g * rz for g in g2], axis=0)
        th_ref[h:h + 1, :] = theta[0:1, :]


def _route(h2, wq_t_bf, k1_bf, k2_bf):
    b, t, d = h2.shape
    nh, nk, half = k1_bf.shape
    qd = wq_t_bf.shape[0]
    tm = _pick(t, (512, 256, 128))
    kern = functools.partial(_route_kernel, topk=PEER_TOPK)
    return pl.pallas_call(
        kern,
        grid=(b, t // tm),
        in_specs=[pl.BlockSpec((None, tm, d), lambda bi, i: (bi, i, 0)),
                  pl.BlockSpec((qd, d), lambda bi, i: (0, 0)),
                  pl.BlockSpec((nh, nk, half), lambda bi, i: (0, 0, 0)),
                  pl.BlockSpec((nh, nk, half), lambda bi, i: (0, 0, 0))],
        out_specs=[pl.BlockSpec((None, nh, nk, tm), lambda bi, i: (bi, 0, 0, i)),
                   pl.BlockSpec((None, nh, nk, tm), lambda bi, i: (bi, 0, 0, i)),
                   pl.BlockSpec((None, nh, tm), lambda bi, i: (bi, 0, i))],
        out_shape=[jax.ShapeDtypeStruct((b, nh, nk, t), F32),
                   jax.ShapeDtypeStruct((b, nh, nk, t), F32),
                   jax.ShapeDtypeStruct((b, nh, t), F32)],
        compiler_params=_params("arbitrary", "arbitrary"),
        name="peer_route",
    )(h2, wq_t_bf, k1_bf, k2_bf)


_GELU_K1 = -2.0 * math.sqrt(2.0 / math.pi) / math.log(2.0)
_GELU_K2 = 0.044715 * _GELU_K1


def _gelu_tanh(x):
    return x / (1.0 + jnp.exp2(x * (_GELU_K1 + _GELU_K2 * (x * x))))


def _peer_kernel(h_ref, dn_ref, up_ref, e1_ref, e2_ref, th_ref, o_ref, p_ref, g_ref, *, lane, n_tiles):
    e = pl.program_id(2)
    nh, nk, tm = e1_ref.shape
    te = dn_ref.shape[0]
    kpt = te // nk
    groups = 8 // kpt

    def build_gates(tile, slot):
        base = pl.multiple_of((tile // groups) * 8, 8)
        sub = tile % groups
        for ii in range(kpt):
            for lc in range(tm // lane):
                ls = slice(lc * lane, (lc + 1) * lane)
                g = jnp.zeros((nk, lane), F32)
                for h in range(nh):
                    grp = e1_ref[h, pl.ds(base, 8), ls]
                    row = grp[ii:ii + 1, :]
                    for s in range(1, groups):
                        row = jnp.where(sub == s, grp[s * kpt + ii:s * kpt + ii + 1, :], row)
                    w = e2_ref[h, :, ls] * row
                    g = g + jnp.where(w >= th_ref[h:h + 1, ls], w, 0.0)
                g_ref[slot, ii * nk:(ii + 1) * nk, ls] = g.astype(g_ref.dtype)

    @pl.when(e == 0)
    def _():
        o_ref[...] = jnp.zeros_like(o_ref)
        build_gates(0, 0)

    build_gates(jnp.minimum(e + 1, n_tiles - 1), (e + 1) % 2)
    act = _gelu_tanh(lax.dot_general(dn_ref[...], h_ref[...], _NT, preferred_element_type=F32))
    p_ref[...] = g_ref[e % 2] * act.astype(p_ref.dtype)
    o_ref[...] += jnp.dot(up_ref[...], p_ref[...], preferred_element_type=F32)


def _peer(h2, e1, e2, th, down_bf, up_t_bf):
    b, t, d = h2.shape
    nh, nk = e1.shape[1], e1.shape[2]
    ne = down_bf.shape[0]
    tm = _pick(t, (1024, 512, 256, 128))
    te = 8 * nk
    assert nk % 8 == 0 and 8 % (te // nk) == 0 and ne == nk * nk
    once = pl.Buffered(1)
    n_tiles = ne // te
    kern = functools.partial(_peer_kernel, lane=128, n_tiles=n_tiles)
    return pl.pallas_call(
        kern,
        grid=(b, t // tm, n_tiles),
        in_specs=[pl.BlockSpec((None, tm, d), lambda bi, i, e: (bi, i, 0), pipeline_mode=once),
                  pl.BlockSpec((te, d), lambda bi, i, e: (e, 0)),
                  pl.BlockSpec((d, te), lambda bi, i, e: (0, e)),
                  pl.BlockSpec((None, nh, nk, tm), lambda bi, i, e: (bi, 0, 0, i), pipeline_mode=once),
                  pl.BlockSpec((None, nh, nk, tm), lambda bi, i, e: (bi, 0, 0, i), pipeline_mode=once),
                  pl.BlockSpec((None, nh, tm), lambda bi, i, e: (bi, 0, i))],
        out_specs=pl.BlockSpec((None, d, tm), lambda bi, i, e: (bi, 0, i), pipeline_mode=once),
        out_shape=jax.ShapeDtypeStruct((b, d, t), F32),
        scratch_shapes=[pltpu.VMEM((te, tm), BF16), pltpu.VMEM((2, te, tm), BF16)],
        compiler_params=_params("arbitrary", "arbitrary", "arbitrary"),
        name="peer_experts",
    )(h2, down_bf, up_t_bf, e1, e2, th)


def _final_kernel(yt_ref, x1_ref, mod_ref, g_ref, o_ref):
    x2 = x1_ref[...] + mod_ref[5:6, :] * yt_ref[...].T
    ms = jnp.mean(x2 * x2, axis=-1, keepdims=True)
    o_ref[...] = x2 * lax.rsqrt(ms + EPS) * g_ref[...]


def _final(y_t, x1, mod, final_gain):
    b, t, d = x1.shape
    tm = _pick(t, (512, 256, 128))
    return pl.pallas_call(
        _final_kernel,
        grid=(b, t // tm),
        in_specs=[pl.BlockSpec((None, d, tm), lambda bi, i: (bi, 0, i)),
                  pl.BlockSpec((None, tm, d), lambda bi, i: (bi, i, 0)),
                  pl.BlockSpec((None, 6, d), lambda bi, i: (bi, 0, 0)),
                  pl.BlockSpec((1, d), lambda bi, i: (0, 0))],
        out_specs=pl.BlockSpec((None, tm, d), lambda bi, i: (bi, i, 0)),
        out_shape=jax.ShapeDtypeStruct((b, t, d), F32),
        compiler_params=_params("arbitrary", "arbitrary"),
        name="final_norm",
    )(y_t, x1, mod, final_gain.reshape(1, d))


def _rope_tables(t, half):
    inv = ROPE_BASE ** (-jnp.arange(half, dtype=F32) / half)
    ang = jnp.arange(t, dtype=F32)[:, None] * inv[None, :]
    return jnp.cos(ang), jnp.sin(ang)


def _trunk(x, mod, w, dims):
    b, t, d = x.shape
    cos, sin = _rope_tables(t, dims["dk"] // 2)
    qkv_v, rqk, rg, gates = _inproj(_hnorm(x, mod, w["norm1_gain"]), cos, sin, w["w_in"], dims)
    na_out = _na(qkv_v, *_na_bias_table(w["na_rel_bias"], t // GRID_W), dims)
    ret_out = _retention(qkv_v, rqk, rg, w["log_gamma"], w["ret_gn_gain"], dims)
    merged = _merge(na_out, ret_out, gates, w["w_na_proj"], w["w_ret_proj"])
    x1, h2 = _outproj(x, merged, w["w_out"], mod, w["norm2_gain"])
    e1, e2, th = _route(h2, w["peer_query_t"], w["peer_keys_1"], w["peer_keys_2"])
    y_t = _peer(h2, e1, e2, th, w["peer_down"], w["peer_up_t"])
    return _final(y_t, x1, mod, w["final_gain"])


def kernel(x_prompt, x_sample, c_prompt, c_sample, ada_w, ada_b, norm1_gain, w_in, na_rel_bias,
           ret_decay_logit, ret_gn_gain, w_na_proj, w_ret_proj, w_out, norm2_gain, peer_query,
           peer_keys_1, peer_keys_2, peer_down, peer_up, final_gain):
    assert ada_w.shape[0] == 1, "the final norm is fused into the single layer's last kernel"
    d = x_prompt.shape[-1]
    na_h = na_rel_bias.shape[1]
    na_w = w_na_proj.shape[1]
    v_w = w_ret_proj.shape[1]
    ret_h = ret_decay_logit.shape[-1]
    qk_w = (w_in.shape[2] - 3 * na_w - 2 * v_w - 2 * d) // 2
    dims = dict(na_h=na_h, na_w=na_w, na_dh=na_w // na_h, ret_h=ret_h, qk_w=qk_w, v_w=v_w,
                dk=qk_w // ret_h, dv=v_w // ret_h, win_h=(na_rel_bias.shape[2] + 1) // 2)
    bp, bs = x_prompt.shape[0], x_sample.shape[0]
    pad = (-(bp + bs)) % 8
    c_all = jnp.concatenate([c_prompt, c_sample, jnp.zeros((pad, d), F32)], axis=0)
    mod = _adaln(c_all, ada_w[0], ada_b[0])
    w = dict(
        norm1_gain=norm1_gain[0], w_in=w_in[0].astype(BF16), na_rel_bias=na_rel_bias[0],
        log_gamma=jax.nn.log_sigmoid(ret_decay_logit[0].astype(F32)), ret_gn_gain=ret_gn_gain[0],
        w_na_proj=w_na_proj[0].astype(BF16), w_ret_proj=w_ret_proj[0].astype(BF16),
        w_out=w_out[0].astype(BF16), norm2_gain=norm2_gain[0],
        peer_query_t=peer_query[0].T.astype(BF16), peer_keys_1=peer_keys_1[0].astype(BF16),
        peer_keys_2=peer_keys_2[0].astype(BF16), peer_down=peer_down[0].astype(BF16),
        peer_up_t=peer_up[0].T.astype(BF16), final_gain=final_gain)
    y_prompt = _trunk(x_prompt, mod[:bp].reshape(bp, 6, d), w, dims)
    y_sample = _trunk(x_sample, mod[bp:bp + bs].reshape(bs, 6, d), w, dims)
    return (y_prompt, y_sample)
```

```python
import functools
import math

import numpy as np
import jax
import jax.numpy as jnp
from jax import lax
from jax.experimental import pallas as pl
from jax.experimental.pallas import tpu as pltpu

GRID_W = 64
PEER_TOPK = 16
ROPE_BASE = 10000.0
EPS = 1e-6
RET_CHUNK = 512
NEG = -1e30

F32 = jnp.float32
BF16 = jnp.bfloat16

VMEM_LIMIT = 56 * 1024 * 1024

_NT = (((1,), (1,)), ((), ()))
_TN = (((0,), (0,)), ((), ()))


def _params(*sem):
    return pltpu.CompilerParams(dimension_semantics=sem, vmem_limit_bytes=VMEM_LIMIT)


def _pick(n, prefs):
    for p in prefs:
        if n % p == 0:
            return p
    return n


def _adaln_kernel(c_ref, w_ref, b_ref, o_ref):
    c = c_ref[...]
    s = c * jax.nn.sigmoid(c)
    o_ref[...] = jnp.dot(s, w_ref[...], preferred_element_type=F32,
                         precision=lax.Precision.HIGHEST) + b_ref[...]


def _adaln(c, ada_w, ada_b):
    m, d = c.shape
    n = ada_w.shape[1]
    tn = _pick(n, (1024, 512, 256, 128))
    return pl.pallas_call(
        _adaln_kernel,
        grid=(n // tn,),
        in_specs=[pl.BlockSpec((m, d), lambda j: (0, 0)),
                  pl.BlockSpec((d, tn), lambda j: (0, j)),
                  pl.BlockSpec((1, tn), lambda j: (0, j))],
        out_specs=pl.BlockSpec((m, tn), lambda j: (0, j)),
        out_shape=jax.ShapeDtypeStruct((m, n), F32),
        compiler_params=_params("arbitrary"),
        name="adaln",
    )(c, ada_w, ada_b.reshape(1, n))


def _modulated_norm(x, gain, shift, scale):
    ms = jnp.mean(x * x, axis=-1, keepdims=True)
    return (x * lax.rsqrt(ms + EPS) * gain) * (1.0 + scale) + shift


def _hnorm_kernel(x_ref, mod_ref, g_ref, o_ref):
    o_ref[...] = _modulated_norm(x_ref[...], g_ref[...], mod_ref[0:1, :], mod_ref[1:2, :]).astype(o_ref.dtype)


def _hnorm(x, mod, gain):
    b, t, d = x.shape
    tm = _pick(t, (512, 256, 128))
    return pl.pallas_call(
        _hnorm_kernel,
        grid=(b, t // tm),
        in_specs=[pl.BlockSpec((None, tm, d), lambda bi, i: (bi, i, 0)),
                  pl.BlockSpec((None, 6, d), lambda bi, i: (bi, 0, 0)),
                  pl.BlockSpec((1, d), lambda bi, i: (0, 0))],
        out_specs=pl.BlockSpec((None, tm, d), lambda bi, i: (bi, i, 0)),
        out_shape=jax.ShapeDtypeStruct((b, t, d), BF16),
        compiler_params=_params("arbitrary", "arbitrary"),
        name="hnorm",
    )(x, mod, gain.reshape(1, d))


def _proj_kernel(*refs, kind, dk):
    h_ref, w_ref = refs[0], refs[1]
    o_ref = refs[-1]
    acc = jnp.dot(h_ref[...], w_ref[...], preferred_element_type=F32)
    if kind == "linear":
        o_ref[...] = (acc * refs[2][...]).astype(o_ref.dtype)
    elif kind == "rotary":
        cs_ref, cos_ref, sin_ref = refs[2:5]
        cos = cos_ref[...]
        sin = sin_ref[...]
        half = dk // 2
        for hh in range(acc.shape[1] // dk):
            lo = slice(hh * dk, hh * dk + half)
            hi = slice(hh * dk + half, (hh + 1) * dk)
            a = acc[:, lo]
            b = acc[:, hi]
            o_ref[:, lo] = ((a * cos - b * sin) * cs_ref[:, lo]).astype(o_ref.dtype)
            o_ref[:, hi] = ((a * sin + b * cos) * cs_ref[:, hi]).astype(o_ref.dtype)
    elif kind == "silu":
        o_ref[...] = (acc * jax.nn.sigmoid(acc)).astype(o_ref.dtype)
    else:
        assert kind == "sigmoid"
        o_ref[...] = jax.nn.sigmoid(acc).astype(o_ref.dtype)


def _proj(h, w_bf, col_block, ncols, tn, kind, extras=(), dk=0):
    b, t, d = h.shape
    tm = _pick(t, (1024, 512, 256, 128))
    extra_specs = []
    for arr in extras:
        if arr.shape[0] == 1:
            extra_specs.append(pl.BlockSpec((1, tn), lambda bi, i, j: (0, j)))
        else:
            extra_specs.append(pl.BlockSpec((tm, arr.shape[1]), lambda bi, i, j: (i, 0)))
    return pl.pallas_call(
        functools.partial(_proj_kernel, kind=kind, dk=dk),
        grid=(b, t // tm, ncols // tn),
        in_specs=[pl.BlockSpec((None, tm, d), lambda bi, i, j: (bi, i, 0)),
                  pl.BlockSpec((d, tn), lambda bi, i, j: (0, col_block(j)))] + extra_specs,
        out_specs=pl.BlockSpec((None, tm, tn), lambda bi, i, j: (bi, i, j)),
        out_shape=jax.ShapeDtypeStruct((b, t, ncols), BF16),
        compiler_params=_params("arbitrary", "arbitrary", "arbitrary"),
        name="inproj_" + kind,
    )(h, w_bf, *extras)


def _inproj(h, cos, sin, w_bf, dims):
    d = h.shape[-1]
    na_w, qk_w, v_w, dk = dims["na_w"], dims["qk_w"], dims["v_w"], dims["dk"]
    tn = 2048
    while any(s % tn for s in (na_w, qk_w, v_w, d)):
        tn //= 2
    assert tn % dk == 0
    n_na, n_qk, n_v = 3 * na_w // tn, 2 * qk_w // tn, v_w // tn
    ones = functools.partial(jnp.ones, dtype=F32)
    cs_lin = jnp.concatenate([jnp.full((na_w,), float(dims["na_dh"]) ** -0.5, F32), ones((2 * na_w + v_w,))])
    cs_rot = jnp.concatenate([ones((qk_w,)), jnp.full((qk_w,), float(dk) ** -0.5, F32)])
    qkv_v = _proj(h, w_bf, lambda j: jnp.where(j < n_na, j, j + n_qk), 3 * na_w + v_w, tn, "linear",
                  (cs_lin.reshape(1, -1),))
    rqk = _proj(h, w_bf, lambda j: n_na + j, 2 * qk_w, tn, "rotary", (cs_rot.reshape(1, -1), cos, sin), dk)
    rg = _proj(h, w_bf, lambda j: n_na + n_qk + n_v + j, v_w, tn, "silu")
    gates = _proj(h, w_bf, lambda j: n_na + n_qk + 2 * n_v + j, 2 * d, tn, "sigmoid")
    return qkv_v, rqk, rg, gates


def _na_plan(rows, win_h):
    kh = min(win_h, rows)
    rpb = _pick(rows, (8, 4, 2, 1))
    kr = min(rows, rpb + kh)
    return kh, rpb, kr


def _na_bias_table(rel_bias, rows):
    nh = rel_bias.shape[0]
    win_h = (rel_bias.shape[1] + 1) // 2
    win_w = (rel_bias.shape[2] + 1) // 2
    kh, rpb, kr = _na_plan(rows, win_h)
    sigs, geoms, ids = [], [], []
    for rb in range(rows // rpb):
        kb = int(np.clip(rb * rpb - (kr - rpb) // 2, 0, rows - kr))
        r = rb * rpb + np.arange(rpb)
        r0 = np.clip(r - kh // 2, 0, rows - kh)
        assert (r0 >= kb).all() and (r0 + kh <= kb + kr).all()
        kabs = kb + np.arange(kr)
        valid = (kabs[None, :] >= r0[:, None]) & (kabs[None, :] < r0[:, None] + kh)
        dr = np.clip(kabs[None, :] - r[:, None] + win_h - 1, 0, 2 * win_h - 2)
        sig = (valid.tobytes(), dr.tobytes())
        if sig not in sigs:
            sigs.append(sig)
            geoms.append((valid, dr))
        ids.append(sigs.index(sig))
    masked = 2 * win_h - 1
    idx = np.stack([np.where(g[0], g[1], masked) for g in geoms])
    assert kr % 2 == 0
    pairs = idx.reshape(len(geoms), rpb, kr // 2, 2)
    combos = sorted({(int(a), int(c)) for a, c in pairs.reshape(-1, 2)})
    pair_ids = np.array([combos.index((int(a), int(c))) for a, c in pairs.reshape(-1, 2)], np.int32)
    cols = np.arange(GRID_W)
    col_start = np.clip(cols - win_w // 2, 0, GRID_W - win_w)
    in_win = (cols[None, :] >= col_start[:, None]) & (cols[None, :] < col_start[:, None] + win_w)
    dc = np.clip(cols[None, :] - cols[:, None] + win_w - 1, 0, 2 * win_w - 2)
    tiles = jnp.where(in_win[None, None], rel_bias.astype(F32)[:, :, dc], NEG)
    tiles = jnp.concatenate([tiles, jnp.full((nh, 1, GRID_W, GRID_W), NEG, F32)], axis=1)
    left = tiles[:, np.array([a for a, _ in combos])]
    right = tiles[:, np.array([c for _, c in combos])]
    tab = jnp.concatenate([left, right], axis=-1)
    return tab, jnp.asarray(ids, jnp.int32), jnp.asarray(pair_ids)


def _na_kernel(vid_ref, pid_ref, q_ref, k_ref, v_ref, b_ref, o_ref, *, rows, rpb, kr, bps):
    step = pl.program_id(2)
    w = GRID_W
    npair = kr // 2
    for u in range(bps):
        rb = step * bps + u
        kb = jnp.clip(rb * rpb - (kr - rpb) // 2, 0, rows - kr)
        start = pl.multiple_of(kb * w, w)
        q = q_ref[u * rpb * w:(u + 1) * rpb * w, :]
        kwin = k_ref[pl.ds(start, kr * w), :]
        vwin = v_ref[pl.ds(start, kr * w), :]
        s = lax.dot_general(q, kwin, _NT, preferred_element_type=F32)
        geom = vid_ref[rb] * (rpb * npair)
        probs, dens = [], []
        for j in range(rpb):
            sj = jnp.concatenate(
                [s[j * w:(j + 1) * w, ip * 2 * w:(ip + 1) * 2 * w] + b_ref[pid_ref[geom + j * npair + ip]]
                 for ip in range(npair)], axis=-1)
            m = jnp.max(sj, axis=-1, keepdims=True)
            p = jnp.exp(sj - m)
            dens.append(jnp.sum(p, axis=-1, keepdims=True))
            probs.append(p.astype(BF16))
        o = jnp.dot(jnp.concatenate(probs, axis=0), vwin, preferred_element_type=F32)
        o_ref[u * rpb * w:(u + 1) * rpb * w, :] = (o / jnp.concatenate(dens, axis=0)).astype(o_ref.dtype)


def _na(proj, bias_tab, block_ids, pair_ids, dims):
    b, t, _ = proj.shape
    nh, dh, na_w = dims["na_h"], dims["na_dh"], dims["na_w"]
    rows = t // GRID_W
    _, rpb, kr = _na_plan(rows, dims["win_h"])
    ncomb = bias_tab.shape[1]
    nblk = rows // rpb
    bps = _pick(nblk, (16, 8, 4, 2, 1))
    hb = na_w // dh
    kern = functools.partial(_na_kernel, rows=rows, rpb=rpb, kr=kr, bps=bps)
    qrows = bps * rpb * GRID_W
    grid_spec = pltpu.PrefetchScalarGridSpec(
        num_scalar_prefetch=2,
        grid=(b, nh, nblk // bps),
        in_specs=[pl.BlockSpec((None, qrows, dh), lambda bi, h, r, vid, pid: (bi, r, h)),
                  pl.BlockSpec((None, t, dh), lambda bi, h, r, vid, pid: (bi, 0, hb + h)),
                  pl.BlockSpec((None, t, dh), lambda bi, h, r, vid, pid: (bi, 0, 2 * hb + h)),
                  pl.BlockSpec((None, ncomb, GRID_W, 2 * GRID_W), lambda bi, h, r, vid, pid: (h, 0, 0, 0))],
        out_specs=pl.BlockSpec((None, qrows, dh), lambda bi, h, r, vid, pid: (bi, r, h)))
    return pl.pallas_call(
        kern,
        grid_spec=grid_spec,
        out_shape=jax.ShapeDtypeStruct((b, t, na_w), BF16),
        compiler_params=_params("arbitrary", "arbitrary", "arbitrary"),
        name="natten",
    )(block_ids, pair_ids, proj, proj, proj, bias_tab)


def _ret_cross(lg, q, k, v, state_ref, qdec_ref, kdec_ref):
    c = q.shape[0]
    qd = (q.astype(F32) * qdec_ref[...]).astype(BF16)
    kd = (k.astype(F32) * kdec_ref[...]).astype(BF16)
    st = state_ref[...]
    out = jnp.dot(qd, st.astype(BF16), preferred_element_type=F32)
    state_ref[...] = st * jnp.exp(c * lg) + lax.dot_general(kd, v, _TN, preferred_element_type=F32)
    return out


def _ret_fwd_kernel(lg_ref, q_ref, k_ref, v_ref, o_ref, state_ref, qdec_ref, kdec_ref, intra_ref):
    hps, c, dk = qdec_ref.shape
    dv = state_ref.shape[-1]
    h0 = pl.program_id(1) * hps

    @pl.when(pl.program_id(2) == 0)
    def _():
        row = lax.broadcasted_iota(jnp.int32, (c, c), 0)
        col = lax.broadcasted_iota(jnp.int32, (c, c), 1)
        pos = lax.broadcasted_iota(jnp.int32, (c, dk), 0).astype(F32)
        for u in range(hps):
            lg, lg_b = lg_ref[0, h0 + u], lg_ref[1, h0 + u]
            intra_ref[u] = jnp.where(row >= col, jnp.exp((row - col).astype(F32) * lg),
                                     jnp.exp((col - row).astype(F32) * lg_b))
            qdec_ref[u] = jnp.exp((pos + 1.0) * lg)
            kdec_ref[u] = jnp.exp((c - 1.0 - pos) * lg)
        state_ref[...] = jnp.zeros_like(state_ref)

    for u in range(hps):
        q = q_ref[:, u * dk:(u + 1) * dk]
        k = k_ref[:, u * dk:(u + 1) * dk]
        v = v_ref[:, u * dv:(u + 1) * dv]
        s = lax.dot_general(q, k, _NT, preferred_element_type=F32) * intra_ref[u]
        o = (jnp.dot(s.astype(BF16), v, preferred_element_type=F32)
             + _ret_cross(lg_ref[0, h0 + u], q, k, v, state_ref.at[u], qdec_ref.at[u], kdec_ref.at[u]))
        o_ref[:, u * dv:(u + 1) * dv] = o.astype(o_ref.dtype)


def _ret_bwd_kernel(lg_ref, q_ref, k_ref, v_ref, f_ref, g_ref, gain_ref, o_ref, state_ref, qdec_ref, kdec_ref):
    hps, c, dk = qdec_ref.shape
    dv = state_ref.shape[-1]
    h0 = pl.program_id(1) * hps

    @pl.when(pl.program_id(2) == 0)
    def _():
        pos = lax.broadcasted_iota(jnp.int32, (c, dk), 0).astype(F32)
        for u in range(hps):
            lg = lg_ref[1, h0 + u]
            qdec_ref[u] = jnp.exp((c - pos) * lg)
            kdec_ref[u] = jnp.exp(pos * lg)
        state_ref[...] = jnp.zeros_like(state_ref)

    for u in range(hps):
        vs = slice(u * dv, (u + 1) * dv)
        q = q_ref[:, u * dk:(u + 1) * dk]
        k = k_ref[:, u * dk:(u + 1) * dk]
        o = (_ret_cross(lg_ref[1, h0 + u], q, k, v_ref[:, vs], state_ref.at[u], qdec_ref.at[u], kdec_ref.at[u])
             + f_ref[:, vs].astype(F32))
        mu = jnp.mean(o, axis=-1, keepdims=True)
        d = o - mu
        var = jnp.mean(d * d, axis=-1, keepdims=True)
        o = d * lax.rsqrt(var + EPS) * gain_ref[:, vs]
        o_ref[:, vs] = (g_ref[:, vs].astype(F32) * o).astype(o_ref.dtype)


def _retention(qkv_v, rqk, rg, log_gamma, gn_gain, dims):
    b, t, _ = rqk.shape
    nh, dk, dv = dims["ret_h"], dims["dk"], dims["dv"]
    na_w, qk_w, v_w = dims["na_w"], dims["qk_w"], dims["v_w"]
    c = _pick(t, (RET_CHUNK, 256, 128))
    nc = t // c
    hps = next(n for n in (4, 2, 1) if nh % n == 0 and (3 * na_w) % (n * dv) == 0)
    scratch = [pltpu.VMEM((hps, dk, dv), F32), pltpu.VMEM((hps, c, dk), F32), pltpu.VMEM((hps, c, dk), F32)]
    kb = nh // hps
    vb = 3 * na_w // (hps * dv)
    smem = pl.BlockSpec(memory_space=pltpu.SMEM)
    fwd = pl.pallas_call(
        _ret_fwd_kernel,
        grid=(b, nh // hps, nc),
        in_specs=[smem,
                  pl.BlockSpec((None, c, hps * dk), lambda bi, h, i: (bi, i, h)),
                  pl.BlockSpec((None, c, hps * dk), lambda bi, h, i: (bi, i, kb + h)),
                  pl.BlockSpec((None, c, hps * dv), lambda bi, h, i: (bi, i, vb + h))],
        out_specs=pl.BlockSpec((None, c, hps * dv), lambda bi, h, i: (bi, i, h)),
        out_shape=jax.ShapeDtypeStruct((b, t, v_w), BF16),
        scratch_shapes=scratch + [pltpu.VMEM((hps, c, c), F32)],
        compiler_params=_params("arbitrary", "arbitrary", "arbitrary"),
        name="ret_fwd",
    )(log_gamma, rqk, rqk, qkv_v)
    return pl.pallas_call(
        _ret_bwd_kernel,
        grid=(b, nh // hps, nc),
        in_specs=[smem,
                  pl.BlockSpec((None, c, hps * dk), lambda bi, h, i: (bi, nc - 1 - i, h)),
                  pl.BlockSpec((None, c, hps * dk), lambda bi, h, i: (bi, nc - 1 - i, kb + h)),
                  pl.BlockSpec((None, c, hps * dv), lambda bi, h, i: (bi, nc - 1 - i, vb + h)),
                  pl.BlockSpec((None, c, hps * dv), lambda bi, h, i: (bi, nc - 1 - i, h)),
                  pl.BlockSpec((None, c, hps * dv), lambda bi, h, i: (bi, nc - 1 - i, h)),
                  pl.BlockSpec((1, hps * dv), lambda bi, h, i: (0, h))],
        out_specs=pl.BlockSpec((None, c, hps * dv), lambda bi, h, i: (bi, nc - 1 - i, h)),
        out_shape=jax.ShapeDtypeStruct((b, t, v_w), BF16),
        scratch_shapes=scratch,
        compiler_params=_params("arbitrary", "arbitrary", "arbitrary"),
        name="ret_bwd",
    )(log_gamma, rqk, rqk, qkv_v, fwd, rg, gn_gain.reshape(1, v_w))


def _merge_kernel(na_ref, ret_ref, wna_ref, wret_ref, gna_ref, gret_ref, o_ref):
    a = jnp.dot(na_ref[...], wna_ref[...], preferred_element_type=F32)
    r = jnp.dot(ret_ref[...], wret_ref[...], preferred_element_type=F32)
    o_ref[...] = (gna_ref[...].astype(F32) * a + gret_ref[...].astype(F32) * r).astype(o_ref.dtype)


def _merge(na_out, ret_out, gates, w_na_bf, w_ret_bf):
    b, t, na_w = na_out.shape
    v_w = ret_out.shape[-1]
    d = w_na_bf.shape[1]
    tm = _pick(t, (1024, 512, 256, 128))
    tn = _pick(d, (512, 256, 128))
    return pl.pallas_call(
        _merge_kernel,
        grid=(b, t // tm, d // tn),
        in_specs=[pl.BlockSpec((None, tm, na_w), lambda bi, i, j: (bi, i, 0)),
                  pl.BlockSpec((None, tm, v_w), lambda bi, i, j: (bi, i, 0)),
                  pl.BlockSpec((na_w, tn), lambda bi, i, j: (0, j)),
                  pl.BlockSpec((v_w, tn), lambda bi, i, j: (0, j)),
                  pl.BlockSpec((None, tm, tn), lambda bi, i, j: (bi, i, j)),
                  pl.BlockSpec((None, tm, tn), lambda bi, i, j: (bi, i, d // tn + j))],
        out_specs=pl.BlockSpec((None, tm, tn), lambda bi, i, j: (bi, i, j)),
        out_shape=jax.ShapeDtypeStruct((b, t, d), BF16),
        compiler_params=_params("arbitrary", "arbitrary", "arbitrary"),
        name="merge",
    )(na_out, ret_out, w_na_bf, w_ret_bf, gates, gates)


def _outproj_kernel(x_ref, m_ref, w_ref, mod_ref, g_ref, x1_ref, h2_ref):
    y = jnp.dot(m_ref[...], w_ref[...], preferred_element_type=F32)
    x1 = x_ref[...] + mod_ref[2:3, :] * y
    x1_ref[...] = x1
    h2_ref[...] = _modulated_norm(x1, g_ref[...], mod_ref[3:4, :], mod_ref[4:5, :]).astype(h2_ref.dtype)


def _outproj(x, merged, w_out_bf, mod, gain2):
    b, t, d = x.shape
    tm = _pick(t, (512, 256, 128))
    return pl.pallas_call(
        _outproj_kernel,
        grid=(b, t // tm),
        in_specs=[pl.BlockSpec((None, tm, d), lambda bi, i: (bi, i, 0)),
                  pl.BlockSpec((None, tm, d), lambda bi, i: (bi, i, 0)),
                  pl.BlockSpec((d, d), lambda bi, i: (0, 0)),
                  pl.BlockSpec((None, 6, d), lambda bi, i: (bi, 0, 0)),
                  pl.BlockSpec((1, d), lambda bi, i: (0, 0))],
        out_specs=[pl.BlockSpec((None, tm, d), lambda bi, i: (bi, i, 0)),
                   pl.BlockSpec((None, tm, d), lambda bi, i: (bi, i, 0))],
        out_shape=[jax.ShapeDtypeStruct((b, t, d), F32), jax.ShapeDtypeStruct((b, t, d), BF16)],
        compiler_params=_params("arbitrary", "arbitrary"),
        name="outproj",
    )(x, merged, w_out_bf, mod, gain2.reshape(1, d))


def _sort_network(n):
    pairs = []

    def merge(lo, size, r):
        step = r * 2
        if step < size:
            merge(lo, size, step)
            merge(lo + r, size, step)
            pairs.extend((i, i + r) for i in range(lo + r, lo + size - r, step))
        else:
            pairs.append((lo, lo + r))

    def sort(lo, size):
        if size > 1:
            sort(lo, size // 2)
            sort(lo + size // 2, size // 2)
            merge(lo, size, 1)

    sort(0, n)
    return pairs


def _exchange(v, i, j):
    v[i], v[j] = jnp.maximum(v[i], v[j]), jnp.minimum(v[i], v[j])


def _top_sorted(groups, k):
    m = len(groups)
    assert m & (m - 1) == 0 and k & (k - 1) == 0
    v = list(groups)
    for i, j in _sort_network(m):
        _exchange(v, i, j)
    v = (v + [jnp.full_like(v[0], -1.0)] * max(k - m, 0))[:k]
    for shift in (4, 2, 1):
        other = [pltpu.roll(x, shift, 0) for x in v]
        v = [jnp.maximum(x, y) for x, y in zip(v, other[::-1])]
        d = k // 2
        while d >= 1:
            for i in range(k):
                if (i // d) % 2 == 0:
                    _exchange(v, i, i + d)
            d //= 2
    return v


def _pack_sublanes(vals, masks):
    counts = [sum(v is u for v in vals) for u in vals]
    base = vals[int(np.argmax(counts))]
    out = base
    for s, v in enumerate(vals):
        if v is not base:
            out = jnp.where(masks[s], v, out)
    return out


def _candidate_slots(topk):
    rows = [[(a, b) for b in range(topk) if (a + 1) * (b + 1) <= topk] for a in range(topk)]
    assert topk == 16
    groups = [rows[0][:8], rows[0][8:], rows[1], rows[2] + rows[4], rows[3] + rows[5] + rows[6],
              rows[7] + [r[0] for r in rows[8:14]], rows[14] + rows[15]]
    assert sorted(p for g in groups for p in g) == sorted(p for r in rows for p in r)
    return [g + [None] * (8 - len(g)) for g in groups]


def _route_kernel(h_ref, wq_ref, k1_ref, k2_ref, e1_ref, e2_ref, th_ref, *, topk):
    nh, nk, half = k1_ref.shape
    tm = h_ref.shape[0]
    qt = lax.dot_general(wq_ref[...], h_ref[...], _NT, preferred_element_type=F32).astype(BF16)
    sub = lax.broadcasted_iota(jnp.int32, (8, tm), 0)
    masks = [sub == s for s in range(8)]
    pad = jnp.full((8, tm), -1.0, F32)
    slots = _candidate_slots(topk)
    for h in range(nh):
        q1 = qt[h * 2 * half:h * 2 * half + half, :]
        q2 = qt[h * 2 * half + half:(h + 1) * 2 * half, :]
        s1 = jnp.dot(k1_ref[h], q1, preferred_element_type=F32)
        s2 = jnp.dot(k2_ref[h], q2, preferred_element_type=F32)
        e1 = jnp.exp(s1 - jnp.max(s1, axis=0, keepdims=True))
        e2 = jnp.exp(s2 - jnp.max(s2, axis=0, keepdims=True))
        g2 = [e2[8 * i:8 * i + 8, :] for i in range(nk // 8)]
        v1 = _top_sorted([e1[8 * i:8 * i + 8, :] for i in range(nk // 8)], topk)
        v2 = _top_sorted(g2, topk)
        a_pk = [_pack_sublanes([pad if p is None else v1[p[0]] for p in g], masks) for g in slots]
        b_pk = [_pack_sublanes([v2[0] if p is None else v2[p[1]] for p in g], masks) for g in slots]
        cand = [a * b for a, b in zip(a_pk, b_pk)]
        best = _top_sorted(cand + [pad], topk)
        z = best[0]
        for w in best[1:]:
            z = z + w
        rz = 1.0 / z
        theta = None
        for a, b, c in zip(a_pk, b_pk, cand):
            t = jnp.where(c >= best[-1], a * (b * rz), 2.0)
            theta = t if theta is None else jnp.minimum(theta, t)
        for shift in (4, 2, 1):
            theta = jnp.minimum(theta, pltpu.roll(theta, shift, 0))
        e1_ref[h] = e1
        e2_ref[h] = jnp.concatenate([g * rz for g in g2], axis=0)
        th_ref[h:h + 1, :] = theta[0:1, :]


def _route(h2, wq_t_bf, k1_bf, k2_bf):
    b, t, d = h2.shape
    nh, nk, half = k1_bf.shape
    qd = wq_t_bf.shape[0]
    tm = _pick(t, (512, 256, 128))
    kern = functools.partial(_route_kernel, topk=PEER_TOPK)
    return pl.pallas_call(
        kern,
        grid=(b, t // tm),
        in_specs=[pl.BlockSpec((None, tm, d), lambda bi, i: (bi, i, 0)),
                  pl.BlockSpec((qd, d), lambda bi, i: (0, 0)),
                  pl.BlockSpec((nh, nk, half), lambda bi, i: (0, 0, 0)),
                  pl.BlockSpec((nh, nk, half), lambda bi, i: (0, 0, 0))],
        out_specs=[pl.BlockSpec((None, nh, nk, tm), lambda bi, i: (bi, 0, 0, i)),
                   pl.BlockSpec((None, nh, nk, tm), lambda bi, i: (bi, 0, 0, i)),
                   pl.BlockSpec((None, nh, tm), lambda bi, i: (bi, 0, i))],
        out_shape=[jax.ShapeDtypeStruct((b, nh, nk, t), F32),
                   jax.ShapeDtypeStruct((b, nh, nk, t), F32),
                   jax.ShapeDtypeStruct((b, nh, t), F32)],
        compiler_params=_params("arbitrary", "arbitrary"),
        name="peer_route",
    )(h2, wq_t_bf, k1_bf, k2_bf)


_GELU_K1 = -2.0 * math.sqrt(2.0 / math.pi) / math.log(2.0)
_GELU_K2 = 0.044715 * _GELU_K1


def _gelu_tanh(x):
    return x / (1.0 + jnp.exp2(x * (_GELU_K1 + _GELU_K2 * (x * x))))


def _peer_kernel(h_ref, dn_ref, up_ref, e1_ref, e2_ref, th_ref, o_ref, p_ref, g_ref, *, lane, n_tiles):
    e = pl.program_id(2)
    nh, nk, tm = e1_ref.shape
    te = dn_ref.shape[0]
    kpt = te // nk
    groups = 8 // kpt

    def build_gates(tile, slot):
        base = pl.multiple_of((tile // groups) * 8, 8)
        sub = tile % groups
        for ii in range(kpt):
            for lc in range(tm // lane):
                ls = slice(lc * lane, (lc + 1) * lane)
                g = jnp.zeros((nk, lane), F32)
                for h in range(nh):
                    grp = e1_ref[h, pl.ds(base, 8), ls]
                    row = grp[ii:ii + 1, :]
                    for s in range(1, groups):
                        row = jnp.where(sub == s, grp[s * kpt + ii:s * kpt + ii + 1, :], row)
                    w = e2_ref[h, :, ls] * row
                    g = g + jnp.where(w >= th_ref[h:h + 1, ls], w, 0.0)
                g_ref[slot, ii * nk:(ii + 1) * nk, ls] = g.astype(g_ref.dtype)

    @pl.when(e == 0)
    def _():
        o_ref[...] = jnp.zeros_like(o_ref)
        build_gates(0, 0)

    build_gates(jnp.minimum(e + 1, n_tiles - 1), (e + 1) % 2)
    pre = lax.dot_general(dn_ref[...], h_ref[...], _NT, preferred_element_type=F32)
    p_ref[...] = g_ref[e % 2] * _gelu_tanh(pre.astype(p_ref.dtype))
    o_ref[...] += jnp.dot(up_ref[...], p_ref[...], preferred_element_type=F32)


def _peer(h2, e1, e2, th, down_bf, up_t_bf):
    b, t, d = h2.shape
    nh, nk = e1.shape[1], e1.shape[2]
    ne = down_bf.shape[0]
    tm = _pick(t, (1024, 512, 256, 128))
    te = 8 * nk
    assert nk % 8 == 0 and 8 % (te // nk) == 0 and ne == nk * nk
    once = pl.Buffered(1)
    n_tiles = ne // te
    kern = functools.partial(_peer_kernel, lane=128, n_tiles=n_tiles)
    return pl.pallas_call(
        kern,
        grid=(b, t // tm, n_tiles),
        in_specs=[pl.BlockSpec((None, tm, d), lambda bi, i, e: (bi, i, 0), pipeline_mode=once),
                  pl.BlockSpec((te, d), lambda bi, i, e: (e, 0)),
                  pl.BlockSpec((d, te), lambda bi, i, e: (0, e)),
                  pl.BlockSpec((None, nh, nk, tm), lambda bi, i, e: (bi, 0, 0, i), pipeline_mode=once),
                  pl.BlockSpec((None, nh, nk, tm), lambda bi, i, e: (bi, 0, 0, i), pipeline_mode=once),
                  pl.BlockSpec((None, nh, tm), lambda bi, i, e: (bi, 0, i))],
        out_specs=pl.BlockSpec((None, d, tm), lambda bi, i, e: (bi, 0, i), pipeline_mode=once),
        out_shape=jax.ShapeDtypeStruct((b, d, t), F32),
        scratch_shapes=[pltpu.VMEM((te, tm), BF16), pltpu.VMEM((2, te, tm), BF16)],
        compiler_params=_params("arbitrary", "arbitrary", "arbitrary"),
        name="peer_experts",
    )(h2, down_bf, up_t_bf, e1, e2, th)


def _final_kernel(yt_ref, x1_ref, mod_ref, g_ref, o_ref):
    x2 = x1_ref[...] + mod_ref[5:6, :] * yt_ref[...].T
    ms = jnp.mean(x2 * x2, axis=-1, keepdims=True)
    o_ref[...] = x2 * lax.rsqrt(ms + EPS) * g_ref[...]


def _final(y_t, x1, mod, final_gain):
    b, t, d = x1.shape
    tm = _pick(t, (512, 256, 128))
    return pl.pallas_call(
        _final_kernel,
        grid=(b, t // tm),
        in_specs=[pl.BlockSpec((None, d, tm), lambda bi, i: (bi, 0, i)),
                  pl.BlockSpec((None, tm, d), lambda bi, i: (bi, i, 0)),
                  pl.BlockSpec((None, 6, d), lambda bi, i: (bi, 0, 0)),
                  pl.BlockSpec((1, d), lambda bi, i: (0, 0))],
        out_specs=pl.BlockSpec((None, tm, d), lambda bi, i: (bi, i, 0)),
        out_shape=jax.ShapeDtypeStruct((b, t, d), F32),
        compiler_params=_params("arbitrary", "arbitrary"),
        name="final_norm",
    )(y_t, x1, mod, final_gain.reshape(1, d))


def _rope_tables(t, half):
    inv = ROPE_BASE ** (-jnp.arange(half, dtype=F32) / half)
    ang = jnp.arange(t, dtype=F32)[:, None] * inv[None, :]
    return jnp.cos(ang), jnp.sin(ang)


def _trunk(x, mod, w, dims):
    b, t, d = x.shape
    cos, sin = _rope_tables(t, dims["dk"] // 2)
    qkv_v, rqk, rg, gates = _inproj(_hnorm(x, mod, w["norm1_gain"]), cos, sin, w["w_in"], dims)
    na_out = _na(qkv_v, *_na_bias_table(w["na_rel_bias"], t // GRID_W), dims)
    ret_out = _retention(qkv_v, rqk, rg, w["log_gamma"], w["ret_gn_gain"], dims)
    merged = _merge(na_out, ret_out, gates, w["w_na_proj"], w["w_ret_proj"])
    x1, h2 = _outproj(x, merged, w["w_out"], mod, w["norm2_gain"])
    e1, e2, th = _route(h2, w["peer_query_t"], w["peer_keys_1"], w["peer_keys_2"])
    y_t = _peer(h2, e1, e2, th, w["peer_down"], w["peer_up_t"])
    return _final(y_t, x1, mod, w["final_gain"])


def kernel(x_prompt, x_sample, c_prompt, c_sample, ada_w, ada_b, norm1_gain, w_in, na_rel_bias,
           ret_decay_logit, ret_gn_gain, w_na_proj, w_ret_proj, w_out, norm2_gain, peer_query,
           peer_keys_1, peer_keys_2, peer_down, peer_up, final_gain):
    assert ada_w.shape[0] == 1, "the final norm is fused into the single layer's last kernel"
    d = x_prompt.shape[-1]
    na_h = na_rel_bias.shape[1]
    na_w = w_na_proj.shape[1]
    v_w = w_ret_proj.shape[1]
    ret_h = ret_decay_logit.shape[-1]
    qk_w = (w_in.shape[2] - 3 * na_w - 2 * v_w - 2 * d) // 2
    dims = dict(na_h=na_h, na_w=na_w, na_dh=na_w // na_h, ret_h=ret_h, qk_w=qk_w, v_w=v_w,
                dk=qk_w // ret_h, dv=v_w // ret_h, win_h=(na_rel_bias.shape[2] + 1) // 2)
    bp, bs = x_prompt.shape[0], x_sample.shape[0]
    pad = (-(bp + bs)) % 8
    c_all = jnp.concatenate([c_prompt, c_sample, jnp.zeros((pad, d), F32)], axis=0)
    mod = _adaln(c_all, ada_w[0], ada_b[0])
    w = dict(
        norm1_gain=norm1_gain[0], w_in=w_in[0].astype(BF16), na_rel_bias=na_rel_bias[0],
        log_gamma=jax.nn.log_sigmoid(ret_decay_logit[0].astype(F32)), ret_gn_gain=ret_gn_gain[0],
        w_na_proj=w_na_proj[0].astype(BF16), w_ret_proj=w_ret_proj[0].astype(BF16),
        w_out=w_out[0].astype(BF16), norm2_gain=norm2_gain[0],
        peer_query_t=peer_query[0].T.astype(BF16), peer_keys_1=peer_keys_1[0].astype(BF16),
        peer_keys_2=peer_keys_2[0].astype(BF16), peer_down=peer_down[0].astype(BF16),
        peer_up_t=peer_up[0].T.astype(BF16), final_gain=final_gain)
    y_prompt = _trunk(x_prompt, mod[:bp].reshape(bp, 6, d), w, dims)
    y_sample = _trunk(x_sample, mod[bp:bp + bs].reshape(bs, 6, d), w, dims)
    return (y_prompt, y_sample)
```
